```python
import math
import jax, jax.numpy as jnp
from jax import lax
import numpy as np

D_MODEL = 1024
BATCH = 16
SEQ = 4096
DEPTH = 1

CTX_LEN = 256
GRID_W = 64
HEAD_DIM = 64
RWKV_WIDTH = D_MODEL // 2
RWKV_HEADS = RWKV_WIDTH // HEAD_DIM
DIFF_WIDTH = D_MODEL - RWKV_WIDTH
DIFF_HEADS = DIFF_WIDTH // (2 * HEAD_DIM)
DECAY_LORA = 64
AAA_LORA = 64
GATE_LORA = 128
DIR_LORA = DECAY_LORA + AAA_LORA + GATE_LORA
RWKV_IN = 3 * RWKV_WIDTH + 2 * DIR_LORA
DIFF_IN = 3 * DIFF_WIDTH
IN_WIDTH = RWKV_IN + DIFF_IN
AXIS_DIM = HEAD_DIM // 2
ROPE_THETA = 10000.0
DIFF_SCALE = HEAD_DIM ** -0.5
N_GROUPS = 4
EXPERTS_PER_GROUP = 8
N_EXPERTS = N_GROUPS * EXPERTS_PER_GROUP
TOP_K_IN_GROUP = 2
EXPERT_FF = 512
ROUTE_BLOCK = 128
ATTN_BLOCK = 128
NORM_EPS = 1e-6
GN_EPS = 64e-5
N_MOD = 6

kernel_name = "hybrid_rwkv7_diffattn_hmoe_dit_block"


def _rms(x, w, eps=NORM_EPS):
    xf = x.astype(jnp.float32)
    y = xf * lax.rsqrt(jnp.mean(xf * xf, axis=-1, keepdims=True) + eps)
    return (y * w).astype(x.dtype)


def _axial_rope_tables(seq_len):
    rows = seq_len // GRID_W
    row_id = jnp.repeat(jnp.arange(rows), GRID_W).astype(jnp.float32)
    col_id = jnp.tile(jnp.arange(GRID_W), rows).astype(jnp.float32)
    inv = ROPE_THETA ** (-jnp.arange(0, AXIS_DIM, 2, dtype=jnp.float32) / AXIS_DIM)
    ar = row_id[:, None] * inv
    ac = col_id[:, None] * inv
    ang = jnp.concatenate([ar, ar, ac, ac], axis=-1)
    return jnp.cos(ang), jnp.sin(ang)


def _rotate_half_axial(x):
    x1, x2, x3, x4 = jnp.split(x, 4, axis=-1)
    return jnp.concatenate([-x2, x1, -x4, x3], axis=-1)


def _apply_rope(x, cos, sin):
    c = cos[:, None, None, :]
    s = sin[:, None, None, :]
    return (x * c + _rotate_half_axial(x) * s).astype(x.dtype)


def _centred_shift(z, w):
    zp = jnp.pad(z, ((0, 0), (1, 1), (0, 0)))
    return w[0] * zp[:, :-2] + w[1] * zp[:, 1:-1] + w[2] * zp[:, 2:]


def _rwkv_direction(z, d, w0, w_up, a0, a_up, g_up, k_k, k_a, r_k):
    B, T, _ = z.shape
    heads = lambda t: t.reshape(B, T, RWKV_HEADS, HEAD_DIM)
    r, k, v, lora = jnp.split(z, [RWKV_WIDTH, 2 * RWKV_WIDTH, 3 * RWKV_WIDTH], axis=-1)
    lora_d = lora[..., d * DIR_LORA:(d + 1) * DIR_LORA]
    lw, la, lg = jnp.split(lora_d, [DECAY_LORA, DECAY_LORA + AAA_LORA], axis=-1)
    kkf = heads(k * k_k).astype(jnp.float32)
    kk = kkf / jnp.maximum(jnp.sqrt(jnp.sum(kkf * kkf, axis=-1, keepdims=True)), 1e-12)
    w_raw = (w0[d] + jnp.tanh(lw) @ w_up[d]).astype(jnp.float32)
    w_log = -jax.nn.softplus(-w_raw) - 0.5
    decay = jnp.exp(-jnp.exp(w_log))
    a = jax.nn.sigmoid(a0[d] + la @ a_up[d])
    g = jax.nn.sigmoid(lg) @ g_up[d]
    kd = k * (1 + (a - 1) * k_a)
    rh, kh, vh, ah = heads(r), heads(kd), heads(v), heads(a)
    bonus = jnp.sum(rh * kh * r_k, axis=-1, keepdims=True) * vh
    scan_in = (rh, heads(decay), kh, vh, -kk, kk * ah)
    return scan_in, g, bonus


def _rwkv7_scan(scan_in, s0, reverse, emit):
    xs = tuple(jnp.moveaxis(t.astype(jnp.float32), 1, 0) for t in scan_in)

    def step(S, inp):
        r, w, k, v, a, b = inp
        sa = jnp.einsum('bhij,bhj->bhi', S, a)
        S = S * w[:, :, None, :] + sa[..., None] * b[:, :, None, :] + v[..., None] * k[:, :, None, :]
        y = jnp.einsum('bhij,bhj->bhi', S, r) if emit else None
        return S, y

    S, ys = lax.scan(step, s0, xs, reverse=reverse)
    return S, (jnp.moveaxis(ys, 0, 1) if emit else None)


def _rwkv_readout(y, bonus, g, ln_w, ln_b):
    B, T = y.shape[:2]
    mu = jnp.mean(y, axis=-1, keepdims=True)
    var = jnp.mean(jnp.square(y - mu), axis=-1, keepdims=True)
    yn = ((y - mu) * lax.rsqrt(var + GN_EPS)).reshape(B, T, RWKV_WIDTH) * ln_w + ln_b
    return ((yn + bonus.reshape(B, T, RWKV_WIDTH)) * g).astype(g.dtype)


def _rwkv_group(rx, rc, w0, w_up, a0, a_up, g_up, k_k, k_a, r_k, ln_w, ln_b, need_ctx_out):
    B = rx.shape[0]
    out_x, out_c = None, None
    for d, reverse in ((0, False), (1, True)):
        sx, gx, bx = _rwkv_direction(rx, d, w0, w_up, a0, a_up, g_up, k_k, k_a, r_k)
        sc, gc, bc = _rwkv_direction(rc, d, w0, w_up, a0, a_up, g_up, k_k, k_a, r_k)
        s0 = jnp.zeros((B, RWKV_HEADS, HEAD_DIM, HEAD_DIM), jnp.float32)
        s_ctx, yc = _rwkv7_scan(sc, s0, reverse, need_ctx_out)
        _, yx = _rwkv7_scan(sx, s_ctx, reverse, True)
        ox = _rwkv_readout(yx, bx, gx, ln_w, ln_b)
        out_x = ox if out_x is None else out_x + ox
        if need_ctx_out:
            oc = _rwkv_readout(yc, bc, gc, ln_w, ln_b)
            out_c = oc if out_c is None else out_c + oc
    return out_x, out_c


def _diff_softmax_attend(q, k, v, lam):
    s = jnp.einsum('bhmqd,bhmkd->bhmqk', q, k, preferred_element_type=jnp.float32) * DIFF_SCALE
    p = jax.nn.softmax(s, axis=-1)
    a = p[:, :, 0] - lam * p[:, :, 1]
    return jnp.einsum('bhqk,bhkd->bhqd', a.astype(v.dtype), v)


def _latent_diff_attention(q, k_all, v_all, lam):
    B, H, _, T, dh = q.shape
    nb = T // ATTN_BLOCK
    qb = q.reshape(B, H, 2, nb, ATTN_BLOCK, dh).transpose(3, 0, 1, 2, 4, 5)
    out = lax.map(lambda qq: _diff_softmax_attend(qq, k_all, v_all, lam), qb)
    return out.transpose(1, 0, 3, 2, 4).reshape(B, T, H, 2 * dh)


def _diff_group(dx, dc, cos, sin, q_norm_w, k_norm_w, lam_q1, lam_k1, lam_q2, lam_k2, subln_w,
                lam_init, need_ctx_out):
    def split_heads(z):
        B, T, _ = z.shape
        q, k, v = jnp.split(z, 3, axis=-1)
        q = _rms(q.reshape(B, T, DIFF_HEADS, 2, HEAD_DIM), q_norm_w)
        k = _rms(k.reshape(B, T, DIFF_HEADS, 2, HEAD_DIM), k_norm_w)
        v = v.reshape(B, T, DIFF_HEADS, 2 * HEAD_DIM)
        return q, k, v

    qx, kx, vx = split_heads(dx)
    qc, kc, vc = split_heads(dc)
    qx = _apply_rope(qx, cos, sin)
    kx = _apply_rope(kx, cos, sin)
    to_bh = lambda z: z.transpose(0, 2, 3, 1, 4)
    k_all = jnp.concatenate([to_bh(kc), to_bh(kx)], axis=3)
    v_all = jnp.concatenate([vc, vx], axis=1).transpose(0, 2, 1, 3)
    lam = (jnp.exp(jnp.sum((lam_q1 * lam_k1).astype(jnp.float32)))
           - jnp.exp(jnp.sum((lam_q2 * lam_k2).astype(jnp.float32))) + lam_init)
    B, T = dx.shape[:2]
    ox = _latent_diff_attention(to_bh(qx), k_all, v_all, lam)
    ox = (_rms(ox, subln_w) * (1.0 - lam_init)).reshape(B, T, DIFF_WIDTH)
    oc = None
    if need_ctx_out:
        Cn = dc.shape[1]
        oc = _diff_softmax_attend(to_bh(qc), to_bh(kc), vc.transpose(0, 2, 1, 3), lam)
        oc = (_rms(oc.transpose(0, 2, 1, 3), subln_w) * (1.0 - lam_init)).reshape(B, Cn, DIFF_WIDTH)
    return ox, oc


def _dispatch_experts(hf, experts, weights, w_gate, w_up, w_down):
    n, d = hf.shape
    m = n * TOP_K_IN_GROUP
    flat_e = experts.reshape(m).astype(jnp.int32)
    flat_w = weights.reshape(m)
    order = jnp.argsort(flat_e)
    sorted_e = flat_e[order]
    tok = (order // TOP_K_IN_GROUP).astype(jnp.int32)
    counts = jnp.bincount(flat_e, length=N_EXPERTS)
    padded = (counts + ROUTE_BLOCK - 1) // ROUTE_BLOCK * ROUTE_BLOCK
    pad_end = jnp.cumsum(padded)
    pad_start = pad_end - padded
    start = jnp.cumsum(counts) - counts
    dest = pad_start[sorted_e] + jnp.arange(m, dtype=jnp.int32) - start[sorted_e]
    n_blocks = (m + N_EXPERTS * (ROUTE_BLOCK - 1) + ROUTE_BLOCK - 1) // ROUTE_BLOCK
    n_slots = n_blocks * ROUTE_BLOCK
    slot_tok = jnp.full((n_slots,), n, jnp.int32).at[dest].set(tok)
    slot_w = jnp.zeros((n_slots,), jnp.float32).at[dest].set(flat_w[order])
    block_expert = jnp.minimum(
        jnp.searchsorted(pad_end, jnp.arange(n_blocks, dtype=jnp.int32) * ROUTE_BLOCK, side='right'),
        N_EXPERTS - 1)
    h_pad = jnp.concatenate([hf, jnp.zeros((1, d), hf.dtype)], axis=0)

    def block_fn(args):
        toks, e = args
        xb = h_pad[toks]
        hid = jax.nn.silu(xb @ w_gate[e]) * (xb @ w_up[e])
        return hid @ w_down[e]

    ys = lax.map(block_fn, (slot_tok.reshape(n_blocks, ROUTE_BLOCK), block_expert))
    contrib = ys.reshape(n_slots, d).astype(jnp.float32) * slot_w[:, None]
    out = jnp.zeros((n + 1, d), jnp.float32).at[slot_tok].add(contrib)
    return out[:n].astype(hf.dtype)


def _hier_moe(h, w_group, b_group, w_expert, b_expert, w_gate, w_up, w_down):
    shape = h.shape
    hf = h.reshape(-1, shape[-1])
    n = hf.shape[0]
    g_logits = (hf @ w_group + b_group).astype(jnp.float32)
    g_sel = jnp.argmax(g_logits, axis=-1).astype(jnp.int32)
    g_prob = jnp.take_along_axis(jax.nn.softmax(g_logits, axis=-1), g_sel[:, None], axis=1)
    e_logits = (hf @ w_expert + b_expert).astype(jnp.float32).reshape(n, N_GROUPS, EXPERTS_PER_GROUP)
    e_logits = jnp.take_along_axis(e_logits, g_sel[:, None, None], axis=1)[:, 0]
    top_v, top_i = lax.top_k(e_logits, TOP_K_IN_GROUP)
    weights = jax.nn.softmax(top_v, axis=-1) * g_prob
    experts = g_sel[:, None] * EXPERTS_PER_GROUP + top_i
    y = _dispatch_experts(hf, experts, weights, w_gate, w_up, w_down)
    return y.reshape(shape)


def setup_inputs(seed: int = 0) -> dict:
    key = jax.random.key(seed)
    keys = jax.random.split(key, 40)
    L, D, RW = DEPTH, D_MODEL, RWKV_WIDTH

    def nrm(j, shape, s):
        return s * jax.random.normal(keys[j], shape, jnp.float32)

    return {
        "x": nrm(0, (BATCH, SEQ, D), 1.0),
        "c": nrm(1, (BATCH, D), 1.0),
        "ctx": nrm(2, (BATCH, CTX_LEN, D), 1.0),
        "c_ctx": nrm(3, (D,), 1.0),
        "norm1_w": 1.0 + nrm(4, (L, D), 0.02),
        "norm2_w": 1.0 + nrm(5, (L, D), 0.02),
        "w_mod": nrm(6, (L, D, N_MOD * D), 0.5 * D ** -0.5),
        "b_mod": nrm(7, (L, N_MOD * D), 0.02),
        "w_in": nrm(8, (L, D, IN_WIDTH), D ** -0.5),
        "shift_w": jnp.array([0.3, 1.0, 0.3], jnp.float32)[None, :, None] + nrm(9, (L, 3, RWKV_IN), 0.05),
        "rwkv_w0": jax.random.uniform(keys[10], (L, 2, RW), jnp.float32, -6.0, -1.0),
        "rwkv_w_up": nrm(11, (L, 2, DECAY_LORA, RW), 0.1),
        "rwkv_a0": nrm(12, (L, 2, RW), 0.1),
        "rwkv_a_up": nrm(13, (L, 2, AAA_LORA, RW), 0.1),
        "rwkv_g_up": nrm(14, (L, 2, GATE_LORA, RW), GATE_LORA ** -0.5),
        "rwkv_k_k": 0.85 + nrm(15, (L, RW), 0.02),
        "rwkv_k_a": 1.0 + nrm(16, (L, RW), 0.02),
        "rwkv_r_k": nrm(17, (L, RWKV_HEADS, HEAD_DIM), 0.1),
        "rwkv_ln_w": 1.0 + nrm(18, (L, RW), 0.02),
        "rwkv_ln_b": nrm(19, (L, RW), 0.02),
        "q_norm_w": 1.0 + nrm(20, (L, HEAD_DIM), 0.02),
        "k_norm_w": 1.0 + nrm(21, (L, HEAD_DIM), 0.02),
        "lam_q1": nrm(22, (L, HEAD_DIM), 0.1),
        "lam_k1": nrm(23, (L, HEAD_DIM), 0.1),
        "lam_q2": nrm(24, (L, HEAD_DIM), 0.1),
        "lam_k2": nrm(25, (L, HEAD_DIM), 0.1),
        "subln_w": 1.0 + nrm(26, (L, 2 * HEAD_DIM), 0.02),
        "w_out": nrm(27, (L, D, D), D ** -0.5),
        "w_group": nrm(28, (L, D, N_GROUPS), D ** -0.5),
        "b_group": nrm(29, (L, N_GROUPS), 0.01),
        "w_expert": nrm(30, (L, D, N_EXPERTS), D ** -0.5),
        "b_expert": nrm(31, (L, N_EXPERTS), 0.01),
        "moe_w_gate": nrm(32, (L, N_EXPERTS, D, EXPERT_FF), D ** -0.5),
        "moe_w_up": nrm(33, (L, N_EXPERTS, D, EXPERT_FF), D ** -0.5),
        "moe_w_down": nrm(34, (L, N_EXPERTS, EXPERT_FF, D), EXPERT_FF ** -0.5),
    }


def reference(x, c, ctx, c_ctx, norm1_w, norm2_w, w_mod, b_mod, w_in, shift_w,
              rwkv_w0, rwkv_w_up, rwkv_a0, rwkv_a_up, rwkv_g_up, rwkv_k_k, rwkv_k_a, rwkv_r_k,
              rwkv_ln_w, rwkv_ln_b, q_norm_w, k_norm_w, lam_q1, lam_k1, lam_q2, lam_k2, subln_w,
              w_out, w_group, b_group, w_expert, b_expert, moe_w_gate, moe_w_up, moe_w_down):
    T = x.shape[1]
    cos, sin = _axial_rope_tables(T)
    for i in range(DEPTH):
        last = i == DEPTH - 1
        lam_init = 0.8 - 0.6 * math.exp(-0.3 * i)
        mod_x = jax.nn.silu(c) @ w_mod[i] + b_mod[i]
        mod_c = jax.nn.silu(c_ctx) @ w_mod[i] + b_mod[i]
        sh1, sc1, g1, sh2, sc2, g2 = jnp.split(mod_x[:, None, :], N_MOD, axis=-1)
        csh1, csc1, cg1, csh2, csc2, cg2 = jnp.split(mod_c, N_MOD, axis=-1)

        hx = _rms(x, norm1_w[i]) * (1 + sc1) + sh1
        hc = _rms(ctx, norm1_w[i]) * (1 + csc1) + csh1
        px = hx @ w_in[i]
        pc = hc @ w_in[i]

        rx = _centred_shift(px[..., :RWKV_IN], shift_w[i])
        rc = _centred_shift(pc[..., :RWKV_IN], shift_w[i])
        o_rx, o_rc = _rwkv_group(rx, rc, rwkv_w0[i], rwkv_w_up[i], rwkv_a0[i], rwkv_a_up[i],
                                 rwkv_g_up[i], rwkv_k_k[i], rwkv_k_a[i], rwkv_r_k[i],
                                 rwkv_ln_w[i], rwkv_ln_b[i], not last)
        o_dx, o_dc = _diff_group(px[..., RWKV_IN:], pc[..., RWKV_IN:], cos, sin, q_norm_w[i],
                                 k_norm_w[i], lam_q1[i], lam_k1[i], lam_q2[i], lam_k2[i],
                                 subln_w[i], lam_init, not last)

        x = x + g1 * (jnp.concatenate([o_rx, o_dx], axis=-1) @ w_out[i])
        h2 = _rms(x, norm2_w[i]) * (1 + sc2) + sh2
        x = x + g2 * _hier_moe(h2, w_group[i], b_group[i], w_expert[i], b_expert[i],
                               moe_w_gate[i], moe_w_up[i], moe_w_down[i])
        if not last:
            ctx = ctx + cg1 * (jnp.concatenate([o_rc, o_dc], axis=-1) @ w_out[i])
            hc2 = _rms(ctx, norm2_w[i]) * (1 + csc2) + csh2
            ctx = ctx + cg2 * _hier_moe(hc2, w_group[i], b_group[i], w_expert[i], b_expert[i],
                                        moe_w_gate[i], moe_w_up[i], moe_w_down[i])
    return x
```

```python
import functools
import math

import jax
import jax.numpy as jnp
from jax import lax
from jax.experimental import pallas as pl
from jax.experimental.pallas import tpu as pltpu

F32 = jnp.float32
BF16 = jnp.bfloat16

D_MODEL = 1024
HEAD_DIM = 64
RWKV_WIDTH = 512
RWKV_HEADS = 8
DIFF_WIDTH = 512
DIFF_HEADS = 4
DIR_LORA = 256
RWKV_IN = 2048
DIFF_IN = 1536
IN_WIDTH = RWKV_IN + DIFF_IN
GRID_W = 64
AXIS_DIM = HEAD_DIM // 2
ROPE_THETA = 10000.0
DIFF_SCALE = HEAD_DIM ** -0.5
N_GROUPS = 4
EXPERTS_PER_GROUP = 8
N_EXPERTS = 32
EXPERT_FF = 512
NORM_EPS = 1e-6
GN_EPS = 64e-5
N_MOD = 6
LAM_INIT = 0.8 - 0.6 * math.exp(-0.3 * 0)

TOK_BLOCK = 256
CHUNK = 64
CHUNKS_PER_BLOCK = TOK_BLOCK // CHUNK
HEAD_PAIRS = RWKV_HEADS // 2
LANES = 128
ROUTE_ROWS = 256
ROUTER_LANES = 128
VMEM_LIMIT = 56 * 1024 * 1024


def _cparams(sem):
    return pltpu.CompilerParams(dimension_semantics=sem, vmem_limit_bytes=VMEM_LIMIT)


def _mod_kernel(c_ref, w_ref, b_ref, o_ref):
    c = c_ref[...]
    s = c * jax.nn.sigmoid(c)
    o_ref[...] = jnp.dot(s, w_ref[...], preferred_element_type=F32,
                         precision=lax.Precision.HIGHEST) + b_ref[...]


def _modulation(cc, w_mod, b_mod):
    rows = cc.shape[0]
    ncol = w_mod.shape[1] // D_MODEL
    return pl.pallas_call(
        _mod_kernel,
        out_shape=jax.ShapeDtypeStruct((rows, w_mod.shape[1]), F32),
        grid=(ncol,),
        in_specs=[pl.BlockSpec((rows, D_MODEL), lambda j: (0, 0)),
                  pl.BlockSpec((D_MODEL, D_MODEL), lambda j: (0, j)),
                  pl.BlockSpec((1, D_MODEL), lambda j: (0, j))],
        out_specs=pl.BlockSpec((rows, D_MODEL), lambda j: (0, j)),
        compiler_params=_cparams(("arbitrary",)),
        name="modulation",
    )(cc, w_mod, b_mod)


def _inproj_kernel(ctx_ref, x_ref, sh_ref, sc_ref, nw_ref, w_ref, pr_ref, pd_ref):
    i = pl.program_id(1)
    xin = jnp.where(i == 0, ctx_ref[0], x_ref[0])
    ms = jnp.mean(xin * xin, axis=-1, keepdims=True)
    y = xin * lax.rsqrt(ms + NORM_EPS) * nw_ref[...]
    h = (y * (1.0 + sc_ref[0]) + sh_ref[0]).astype(BF16)
    p = jnp.dot(h, w_ref[...], preferred_element_type=F32)
    pr_ref[0] = p[:, :RWKV_IN].astype(BF16)
    pd_ref[0] = p[:, RWKV_IN:].astype(BF16)


def _mod_row(nb):
    return lambda b, i: jnp.where(i == 0, nb, b)


def _inproj(ctx, x, mod3, norm1_w, w_in_bf):
    B, T, D = x.shape
    nblk = 1 + T // TOK_BLOCK
    S = nblk * TOK_BLOCK
    row = _mod_row(B)
    return pl.pallas_call(
        _inproj_kernel,
        out_shape=(jax.ShapeDtypeStruct((B, S, RWKV_IN), BF16),
                   jax.ShapeDtypeStruct((B, S, DIFF_IN), BF16)),
        grid=(B, nblk),
        in_specs=[pl.BlockSpec((1, TOK_BLOCK, D), lambda b, i: (b, 0, 0)),
                  pl.BlockSpec((1, TOK_BLOCK, D), lambda b, i: (b, jnp.maximum(i - 1, 0), 0)),
                  pl.BlockSpec((1, 1, D), lambda b, i: (row(b, i), 0, 0)),
                  pl.BlockSpec((1, 1, D), lambda b, i: (row(b, i), 0, 1)),
                  pl.BlockSpec((1, D), lambda b, i: (0, 0)),
                  pl.BlockSpec((D, IN_WIDTH), lambda b, i: (0, 0))],
        out_specs=(pl.BlockSpec((1, TOK_BLOCK, RWKV_IN), lambda b, i: (b, i, 0)),
                   pl.BlockSpec((1, TOK_BLOCK, DIFF_IN), lambda b, i: (b, i, 0))),
        compiler_params=_cparams(("parallel", "arbitrary")),
        name="inproj",
    )(ctx, x, mod3, mod3, norm1_w, w_in_bf)


def _head_sum(v, e_ref):
    return jnp.dot(v, e_ref[...], preferred_element_type=F32)


def _rwkv_prep_kernel(nblk, z_ref, zp_ref, zn_ref, sw_ref, w0_ref, wup_ref, a0_ref, aup_ref,
                      gup_ref, kk_ref, ka_ref, rk_ref, e_ref,
                      r_ref, v_ref, kn_ref, lw_ref, kd_ref, bb_ref, g_ref, bo_ref):
    i = pl.program_id(1)
    z = z_ref[0].astype(F32)
    row = lax.broadcasted_iota(jnp.int32, (TOK_BLOCK, 1), 0)
    prev_ok = jnp.logical_and(i != 0, i != 1)
    next_ok = jnp.logical_and(i != 0, i != nblk - 1)
    zp = zp_ref[0][15:16, :].astype(F32) * prev_ok.astype(F32)
    zn = zn_ref[0][0:1, :].astype(F32) * next_ok.astype(F32)
    prev = jnp.where(row == 0, zp, pltpu.roll(z, 1, 0))
    nxt = jnp.where(row == TOK_BLOCK - 1, zn, pltpu.roll(z, TOK_BLOCK - 1, 0))
    sw = sw_ref[...]
    rx = sw[0:1] * prev + sw[1:2] * z + sw[2:3] * nxt
    r = rx[:, 0:512]
    k = rx[:, 512:1024]
    v = rx[:, 1024:1536]
    kkf = k * kk_ref[...]
    nrm = jnp.sqrt(_head_sum(kkf * kkf, e_ref))
    kn = kkf / jnp.maximum(nrm, 1e-12)
    for p in range(HEAD_PAIRS):
        sl = slice(p * LANES, (p + 1) * LANES)
        r_ref[0, p] = r[:, sl].astype(BF16)
        v_ref[0, p] = v[:, sl].astype(BF16)
        kn_ref[0, p] = kn[:, sl].astype(BF16)
    for d in range(2):
        base = 1536 + d * DIR_LORA
        x128 = rx[:, base:base + 128]
        lg = rx[:, base + 128:base + 256]
        w_raw = w0_ref[d] + jnp.dot(jnp.tanh(x128), wup_ref[d], preferred_element_type=F32)
        u = -w_raw
        sp = jnp.maximum(u, 0.0) + jnp.log(1.0 + jnp.exp(-jnp.abs(u)))
        logw = -jnp.exp(-sp - 0.5)
        a = jax.nn.sigmoid(a0_ref[d] + jnp.dot(x128, aup_ref[d], preferred_element_type=F32))
        g = jnp.dot(jax.nn.sigmoid(lg), gup_ref[d], preferred_element_type=F32)
        kd = k * (1.0 + (a - 1.0) * ka_ref[...])
        bonus = _head_sum(r * kd * rk_ref[...], e_ref) * v
        bb = kn * a
        for p in range(HEAD_PAIRS):
            sl = slice(p * LANES, (p + 1) * LANES)
            lw_ref[d, 0, p] = logw[:, sl]
            kd_ref[d, 0, p] = kd[:, sl].astype(BF16)
            bb_ref[d, 0, p] = bb[:, sl].astype(BF16)
        g_ref[d, 0] = g.astype(BF16)
        bo_ref[d, 0] = bonus.astype(BF16)


def _rwkv_prep(pr, shift_w, w0, wup_pad, a0, aup_pad, g_up, k_k, k_a, r_k, e512):
    B, S, _ = pr.shape
    nblk = S // TOK_BLOCK
    hb = TOK_BLOCK // 16
    nh = S // 16
    pair = lambda dt: jax.ShapeDtypeStruct((B, HEAD_PAIRS, S, LANES), dt)
    pair2 = lambda dt: jax.ShapeDtypeStruct((2, B, HEAD_PAIRS, S, LANES), dt)
    chan2 = jax.ShapeDtypeStruct((2, B, S, RWKV_WIDTH), BF16)
    pspec = pl.BlockSpec((1, HEAD_PAIRS, TOK_BLOCK, LANES), lambda b, i: (b, 0, i, 0))
    pspec2 = pl.BlockSpec((2, 1, HEAD_PAIRS, TOK_BLOCK, LANES), lambda b, i: (0, b, 0, i, 0))
    cspec2 = pl.BlockSpec((2, 1, TOK_BLOCK, RWKV_WIDTH), lambda b, i: (0, b, i, 0))
    full = lambda a: pl.BlockSpec(a.shape, lambda b, i: (0,) * a.ndim)
    return pl.pallas_call(
        functools.partial(_rwkv_prep_kernel, nblk),
        out_shape=(pair(BF16), pair(BF16), pair(BF16), pair2(F32), pair2(BF16), pair2(BF16),
                   chan2, chan2),
        grid=(B, nblk),
        in_specs=[pl.BlockSpec((1, TOK_BLOCK, RWKV_IN), lambda b, i: (b, i, 0)),
                  pl.BlockSpec((1, 16, RWKV_IN), lambda b, i: (b, jnp.maximum(i * hb - 1, 0), 0)),
                  pl.BlockSpec((1, 16, RWKV_IN), lambda b, i: (b, jnp.minimum((i + 1) * hb, nh - 1), 0)),
                  full(shift_w), full(w0), full(wup_pad), full(a0), full(aup_pad), full(g_up),
                  full(k_k), full(k_a), full(r_k), full(e512)],
        out_specs=(pspec, pspec, pspec, pspec2, pspec2, pspec2, cspec2, cspec2),
        compiler_params=_cparams(("parallel", "arbitrary")),
        name="rwkv_prep",
    )(pr, pr, pr, shift_w, w0, wup_pad, a0, aup_pad, g_up, k_k, k_a, r_k, e512)


def _bmm(a, b):
    return jnp.einsum('nij,njk->nik', a, b, preferred_element_type=F32)


def _bmm_nt(a, b):
    return jnp.einsum('nij,nkj->nik', a, b, preferred_element_type=F32)


def _rwkv_scan_kernel(reverse, r_ref, v_ref, kn_ref, lw_ref, kd_ref, bb_ref, y_ref,
                      h_ref, q_s, y0_s, m_s, n_s, pt_s):
    C = CHUNK
    NC = CHUNKS_PER_BLOCK

    @pl.when(pl.program_id(1) == 0)
    def _():
        h_ref[...] = jnp.zeros_like(h_ref)

    t_i = lax.broadcasted_iota(jnp.int32, (C, C), 0)
    s_i = lax.broadcasted_iota(jnp.int32, (C, C), 1)
    if reverse:
        strict = s_i > t_i
        incl = s_i >= t_i
    else:
        strict = s_i < t_i
        incl = s_i <= t_i
    blk16 = (t_i // 16) == (s_i // 16)
    blk32 = (t_i // 32) == (s_i // 32)
    eye = (t_i == s_i).astype(F32)
    tri = jnp.broadcast_to(incl.astype(F32)[None], (NC, C, C))
    lane = lax.broadcasted_iota(jnp.int32, (1, 1, LANES), 2)
    head0 = lane < HEAD_DIM
    rr = lax.broadcasted_iota(jnp.int32, (LANES, LANES), 0)
    cc = lax.broadcasted_iota(jnp.int32, (LANES, LANES), 1)
    same_head = ((rr // HEAD_DIM) == (cc // HEAD_DIM))[None]

    def chunked(blk):
        return blk.astype(F32).reshape(NC, C, LANES)

    def pair_body(p, carry):
        r = chunked(r_ref[0, p])
        v = chunked(v_ref[0, p])
        kn = chunked(kn_ref[0, p])
        kd = chunked(kd_ref[0, 0, p])
        bb = chunked(bb_ref[0, 0, p])
        logw = lw_ref[0, 0, p].reshape(NC, C, LANES)
        lw_hi = logw.astype(BF16).astype(F32)
        lw_lo = logw - lw_hi
        L = _bmm(tri, lw_hi) + _bmm(tri, lw_lo)
        Lex = L - logw
        tot = L[:, 0:1, :] if reverse else L[:, C - 1:C, :]
        e_in = jnp.exp(-L)
        e_hat = jnp.exp(tot - L)
        at = -kn * jnp.exp(Lex)
        rt = r * jnp.exp(L)
        bt = bb * e_in
        kt = kd * e_in
        bh = bb * e_hat
        kh = kd * e_hat
        lhs = jnp.concatenate([at, rt], axis=1)
        rhs = jnp.concatenate([bt, kt], axis=1)
        sc = jnp.concatenate([_bmm_nt(jnp.where(head0, lhs, 0.0), rhs),
                              _bmm_nt(jnp.where(head0, 0.0, lhs), rhs)], axis=0)
        a_ab = jnp.where(strict, sc[:, :C, :C], 0.0)
        a_ak = jnp.where(strict, sc[:, :C, C:], 0.0)
        a_rb = jnp.where(incl, sc[:, C:, :C], 0.0)
        a_rk = jnp.where(incl, sc[:, C:, C:], 0.0)
        nd = jnp.where(blk16, a_ab, 0.0)
        T = eye + nd
        pw = _bmm(nd, nd)
        T = T + _bmm(pw, T)
        pw = _bmm(pw, pw)
        T = T + _bmm(pw, T)
        pw = _bmm(pw, pw)
        T = T + _bmm(pw, T)
        off = jnp.where(jnp.logical_and(blk32, jnp.logical_not(blk16)), a_ab, 0.0)
        T = T + _bmm(T, _bmm(off, T))
        off = jnp.where(blk32, 0.0, a_ab)
        T = T + _bmm(T, _bmm(off, T))
        v2 = jnp.concatenate([v, v], axis=0)
        at2 = jnp.concatenate([at, at], axis=0)
        akv = _bmm(a_ak, v2)
        arkv = _bmm(a_rk, v2)
        w1 = _bmm(T, at2)
        u0 = _bmm(T, akv)
        qh = _bmm(a_rb, w1)
        y0 = _bmm(a_rb, u0) + arkv
        merge = lambda zz: jnp.where(head0, zz[:NC], zz[NC:])
        w1p = merge(w1)
        u0p = merge(u0)
        q_s[...] = rt + merge(qh)
        y0_s[...] = merge(y0)
        bht = jnp.swapaxes(bh, 1, 2)
        kht = jnp.swapaxes(kh, 1, 2)
        m_s[...] = jnp.where(same_head, _bmm(bht, w1p), 0.0)
        n_s[...] = jnp.where(same_head, _bmm(bht, u0p) + _bmm(kht, v), 0.0)
        pt_s[...] = jnp.exp(jnp.sum(jnp.swapaxes(logw, 1, 2), axis=2, keepdims=True))
        for j in range(NC):
            c = NC - 1 - j if reverse else j
            H = h_ref[p]
            y = jnp.dot(q_s[c], H, preferred_element_type=F32) + y0_s[c]
            y_ref[0, p, c * C:(c + 1) * C, :] = y
            h_ref[p] = pt_s[c] * H + jnp.dot(m_s[c], H, preferred_element_type=F32) + n_s[c]
        return carry

    lax.fori_loop(0, HEAD_PAIRS, pair_body, 0)


def _rwkv_scan(d, r, v, kn, lw, kd, bb):
    B, _, S, _ = r.shape
    nblk = S // TOK_BLOCK
    reverse = d == 1
    if reverse:
        blk = lambda i: jnp.where(i == 0, 0, nblk - i)
    else:
        blk = lambda i: i
    pspec = pl.BlockSpec((1, HEAD_PAIRS, TOK_BLOCK, LANES), lambda b, i: (b, 0, blk(i), 0))
    pspec_d = pl.BlockSpec((1, 1, HEAD_PAIRS, TOK_BLOCK, LANES), lambda b, i: (d, b, 0, blk(i), 0))
    NC, C = CHUNKS_PER_BLOCK, CHUNK
    return pl.pallas_call(
        functools.partial(_rwkv_scan_kernel, reverse),
        out_shape=jax.ShapeDtypeStruct((B, HEAD_PAIRS, S, LANES), F32),
        grid=(B, nblk),
        in_specs=[pspec, pspec, pspec, pspec_d, pspec_d, pspec_d],
        out_specs=pspec,
        scratch_shapes=[pltpu.VMEM((HEAD_PAIRS, LANES, LANES), F32),
                        pltpu.VMEM((NC, C, LANES), F32),
                        pltpu.VMEM((NC, C, LANES), F32),
                        pltpu.VMEM((NC, LANES, LANES), F32),
                        pltpu.VMEM((NC, LANES, LANES), F32),
                        pltpu.VMEM((NC, LANES, 1), F32)],
        compiler_params=_cparams(("parallel", "arbitrary")),
        name="rwkv_scan_rev" if reverse else "rwkv_scan_fwd",
    )(r, v, kn, lw, kd, bb)


def _rwkv_readout_kernel(y0_ref, y1_ref, g_ref, bo_ref, lnw_ref, lnb_ref, e_ref, o_ref):
    inv = 1.0 / HEAD_DIM
    for p in range(HEAD_PAIRS):
        sl = slice(p * LANES, (p + 1) * LANES)
        acc = None
        for d, y_ref in enumerate((y0_ref, y1_ref)):
            y = y_ref[0, p]
            mu = _head_sum(y, e_ref) * inv
            dl = y - mu
            var = _head_sum(dl * dl, e_ref) * inv
            yn = dl * lax.rsqrt(var + GN_EPS) * lnw_ref[:, sl] + lnb_ref[:, sl]
            o = (yn + bo_ref[d, 0, :, sl].astype(F32)) * g_ref[d, 0, :, sl].astype(F32)
            acc = o if acc is None else acc + o
        o_ref[0, :, sl] = acc.astype(BF16)


def _rwkv_readout(y0, y1, g, bonus, ln_w, ln_b, e128):
    B, _, S, _ = y0.shape
    T = S - TOK_BLOCK
    yspec = pl.BlockSpec((1, HEAD_PAIRS, TOK_BLOCK, LANES), lambda b, i: (b, 0, i + 1, 0))
    cspec2 = pl.BlockSpec((2, 1, TOK_BLOCK, RWKV_WIDTH), lambda b, i: (0, b, i + 1, 0))
    full = lambda a: pl.BlockSpec(a.shape, lambda b, i: (0,) * a.ndim)
    return pl.pallas_call(
        _rwkv_readout_kernel,
        out_shape=jax.ShapeDtypeStruct((B, T, RWKV_WIDTH), BF16),
        grid=(B, T // TOK_BLOCK),
        in_specs=[yspec, yspec, cspec2, cspec2, full(ln_w), full(ln_b), full(e128)],
        out_specs=pl.BlockSpec((1, TOK_BLOCK, RWKV_WIDTH), lambda b, i: (b, i, 0)),
        compiler_params=_cparams(("parallel", "arbitrary")),
        name="rwkv_readout",
    )(y0, y1, g, bonus, ln_w, ln_b, e128)


def _qk_prep_kernel(z_ref, nw_ref, cos_ref, sin_ref, e_ref, o_ref):
    z = z_ref[0].astype(F32)
    ms = _head_sum(z * z, e_ref) * (1.0 / HEAD_DIM)
    zn = z * lax.rsqrt(ms + NORM_EPS) * nw_ref[...]
    cos = cos_ref[...]
    sin = sin_ref[...]
    lane = lax.broadcasted_iota(jnp.int32, (1, LANES), 1)
    first = (lane % (2 * 16)) < 16
    for j in range(2 * DIFF_WIDTH // LANES):
        sl = slice(j * LANES, (j + 1) * LANES)
        t = zn[:, sl]
        rot = jnp.where(first, -pltpu.roll(t, LANES - 16, 1), pltpu.roll(t, 16, 1))
        out = t * cos + rot * sin
        if j < DIFF_WIDTH // LANES:
            out = out * DIFF_SCALE
        o_ref[0, :, sl] = out.astype(BF16)


def _qk_prep(pd, qk_w, cos, sin, e1024):
    B, S, _ = pd.shape
    full = lambda a: pl.BlockSpec(a.shape, lambda b, i: (0,) * a.ndim)
    return pl.pallas_call(
        _qk_prep_kernel,
        out_shape=jax.ShapeDtypeStruct((B, S, 2 * DIFF_WIDTH), BF16),
        grid=(B, S // TOK_BLOCK),
        in_specs=[pl.BlockSpec((1, TOK_BLOCK, 2 * DIFF_WIDTH), lambda b, i: (b, i, 0)),
                  full(qk_w),
                  pl.BlockSpec((TOK_BLOCK, LANES), lambda b, i: (i, 0)),
                  pl.BlockSpec((TOK_BLOCK, LANES), lambda b, i: (i, 0)),
                  full(e1024)],
        out_specs=pl.BlockSpec((1, TOK_BLOCK, 2 * DIFF_WIDTH), lambda b, i: (b, i, 0)),
        compiler_params=_cparams(("parallel", "arbitrary")),
        name="qk_prep",
    )(pd, qk_w, cos, sin, e1024)


def _diff_attn_kernel(nkv, q_ref, k_ref, v_ref, lam_ref, sw_ref, o_ref):
    TQ = TOK_BLOCK
    q = q_ref[0]
    lane = lax.broadcasted_iota(jnp.int32, (1, LANES), 1)
    zero = jnp.zeros_like(q)
    qs = jnp.concatenate([jnp.where(lane < HEAD_DIM, q, zero),
                          jnp.where(lane < HEAD_DIM, zero, q)], axis=0)

    def body(j, carry):
        m, l, acc = carry
        kj = k_ref[0, pl.ds(pl.multiple_of(j * TOK_BLOCK, TOK_BLOCK), TOK_BLOCK), :]
        vj = v_ref[0, pl.ds(pl.multiple_of(j * TOK_BLOCK, TOK_BLOCK), TOK_BLOCK), :]
        s = lax.dot_general(qs, kj, (((1,), (1,)), ((), ())), preferred_element_type=F32)
        m_new = jnp.maximum(m, jnp.max(s, axis=1, keepdims=True))
        alpha = jnp.exp(m - m_new)
        p = jnp.exp(s - m_new)
        l = alpha * l + jnp.sum(p, axis=1, keepdims=True)
        acc = alpha * acc + jnp.dot(p.astype(BF16), vj, preferred_element_type=F32)
        return m_new, l, acc

    m0 = jnp.full((2 * TQ, 1), -1e30, F32)
    l0 = jnp.zeros((2 * TQ, 1), F32)
    a0 = jnp.zeros((2 * TQ, LANES), F32)
    m, l, acc = lax.fori_loop(0, nkv, body, (m0, l0, a0))
    o = acc / l
    lp = lam_ref[...]
    lam = (jnp.exp(jnp.sum(lp[0:1] * lp[1:2], axis=1, keepdims=True))
           - jnp.exp(jnp.sum(lp[2:3] * lp[3:4], axis=1, keepdims=True)) + LAM_INIT)
    od = o[:TQ] - lam * o[TQ:]
    ms = jnp.mean(od * od, axis=-1, keepdims=True)
    o_ref[0] = (od * lax.rsqrt(ms + NORM_EPS) * sw_ref[...] * (1.0 - LAM_INIT)).astype(BF16)


def _diff_attn(qk, pd, lam_p, subln_w):
    B, S, _ = qk.shape
    T = S - TOK_BLOCK
    H = DIFF_HEADS
    return pl.pallas_call(
        functools.partial(_diff_attn_kernel, S // TOK_BLOCK),
        out_shape=jax.ShapeDtypeStruct((B, T, DIFF_WIDTH), BF16),
        grid=(B, H, T // TOK_BLOCK),
        in_specs=[pl.BlockSpec((1, TOK_BLOCK, LANES), lambda b, h, i: (b, i + 1, h)),
                  pl.BlockSpec((1, S, LANES), lambda b, h, i: (b, 0, H + h)),
                  pl.BlockSpec((1, S, LANES), lambda b, h, i: (b, 0, 2 * H + h)),
                  pl.BlockSpec(lam_p.shape, lambda b, h, i: (0, 0)),
                  pl.BlockSpec(subln_w.shape, lambda b, h, i: (0, 0))],
        out_specs=pl.BlockSpec((1, TOK_BLOCK, LANES), lambda b, h, i: (b, i, h)),
        compiler_params=_cparams(("parallel", "parallel", "arbitrary")),
        name="diff_attn",
    )(qk, qk, pd, lam_p, subln_w)


def _outproj_kernel(or_ref, od_ref, x_ref, wr_ref, wd_ref, g1_ref, sh_ref, sc_ref, nw_ref,
                    rwh_ref, rwl_ref, rb_ref, x1_ref, h2_ref, rt_ref):
    attn = (jnp.dot(or_ref[0], wr_ref[...], preferred_element_type=F32)
            + jnp.dot(od_ref[0], wd_ref[...], preferred_element_type=F32))
    x1 = x_ref[0] + g1_ref[0] * attn
    x1_ref[0] = x1
    ms = jnp.mean(x1 * x1, axis=-1, keepdims=True)
    h2 = x1 * lax.rsqrt(ms + NORM_EPS) * nw_ref[...] * (1.0 + sc_ref[0]) + sh_ref[0]
    h2_ref[0] = h2
    hh = h2.astype(BF16)
    hl = (h2 - hh.astype(F32)).astype(BF16)
    logits = (jnp.dot(hh, rwh_ref[...], preferred_element_type=F32)
              + jnp.dot(hl, rwh_ref[...], preferred_element_type=F32)
              + jnp.dot(hh, rwl_ref[...], preferred_element_type=F32)) + rb_ref[...]
    lane = lax.broadcasted_iota(jnp.int32, logits.shape, 1).astype(F32)
    neg = -jnp.inf
    big = 1e9
    gl = jnp.where(lane < N_GROUPS, logits, neg)
    gmax = jnp.max(gl, axis=1, keepdims=True)
    gsel = jnp.min(jnp.where(gl == gmax, lane, big), axis=1, keepdims=True)
    gprob = 1.0 / jnp.sum(jnp.exp(gl - gmax), axis=1, keepdims=True)
    lo = N_GROUPS + EXPERTS_PER_GROUP * gsel
    el = jnp.where(jnp.logical_and(lane >= lo, lane < lo + EXPERTS_PER_GROUP), logits, neg)
    v1 = jnp.max(el, axis=1, keepdims=True)
    i1 = jnp.min(jnp.where(el == v1, lane, big), axis=1, keepdims=True)
    el2 = jnp.where(lane == i1, neg, el)
    v2 = jnp.max(el2, axis=1, keepdims=True)
    i2 = jnp.min(jnp.where(el2 == v2, lane, big), axis=1, keepdims=True)
    e21 = jnp.exp(v2 - v1)
    w1 = gprob / (1.0 + e21)
    w2 = gprob * e21 / (1.0 + e21)
    rt_ref[0] = jnp.where(lane == 0, i1 - N_GROUPS,
                          jnp.where(lane == 1, i2 - N_GROUPS,
                                    jnp.where(lane == 2, w1, jnp.where(lane == 3, w2, 0.0))))


def _outproj(o_r, o_d, x, w_out_r, w_out_d, mod3, norm2_w, rw_hi, rw_lo, rb):
    B, T, D = x.shape
    full = lambda a: pl.BlockSpec(a.shape, lambda b, i: (0,) * a.ndim)
    modspec = lambda col: pl.BlockSpec((1, 1, D), lambda b, i: (b, 0, col))
    tok = lambda w: pl.BlockSpec((1, TOK_BLOCK, w), lambda b, i: (b, i, 0))
    return pl.pallas_call(
        _outproj_kernel,
        out_shape=(jax.ShapeDtypeStruct((B, T, D), F32),
                   jax.ShapeDtypeStruct((B, T, D), F32),
                   jax.ShapeDtypeStruct((B, T, ROUTER_LANES), F32)),
        grid=(B, T // TOK_BLOCK),
        in_specs=[tok(RWKV_WIDTH), tok(DIFF_WIDTH), tok(D), full(w_out_r), full(w_out_d),
                  modspec(2), modspec(3), modspec(4), full(norm2_w),
                  full(rw_hi), full(rw_lo), full(rb)],
        out_specs=(tok(D), tok(D), tok(ROUTER_LANES)),
        compiler_params=_cparams(("parallel", "arbitrary")),
        name="outproj_router",
    )(o_r, o_d, x, w_out_r, w_out_d, mod3, mod3, mod3, norm2_w, rw_hi, rw_lo, rb)


def _row_copy(src_ref, src_row, dst_ref, dst_row, sem):
    return pltpu.make_async_copy(src_ref.at[pl.ds(src_row, 1), :], dst_ref.at[pl.ds(dst_row, 1), :], sem)


def _moe_scatter_kernel(dest_ref, h_ref, xs_in_ref, xs_ref, sem):
    del xs_in_ref

    def start(t, c):
        _row_copy(h_ref, t, xs_ref, dest_ref[0, 0, 2 * t], sem).start()
        _row_copy(h_ref, t, xs_ref, dest_ref[0, 0, 2 * t + 1], sem).start()
        return c

    def wait(t, c):
        _row_copy(h_ref, t, xs_ref, dest_ref[0, 0, 2 * t], sem).wait()
        _row_copy(h_ref, t, xs_ref, dest_ref[0, 0, 2 * t + 1], sem).wait()
        return c

    lax.fori_loop(0, TOK_BLOCK, start, 0)
    lax.fori_loop(0, TOK_BLOCK, wait, 0)


def _moe_scatter(dest3, h2, xs0):
    N, D = h2.shape
    return pl.pallas_call(
        _moe_scatter_kernel,
        out_shape=jax.ShapeDtypeStruct(xs0.shape, xs0.dtype),
        grid=(N // TOK_BLOCK,),
        in_specs=[pl.BlockSpec((1, 1, 2 * TOK_BLOCK), lambda i: (i, 0, 0), memory_space=pltpu.SMEM),
                  pl.BlockSpec((TOK_BLOCK, D), lambda i: (i, 0)),
                  pl.BlockSpec(memory_space=pl.ANY)],
        out_specs=pl.BlockSpec(memory_space=pl.ANY),
        scratch_shapes=[pltpu.SemaphoreType.DMA],
        input_output_aliases={2: 0},
        compiler_params=_cparams(("arbitrary",)),
        name="moe_scatter",
    )(dest3, h2, xs0)


def _moe_expert_kernel(be_ref, x_ref, wg_ref, wu_ref, wd_ref, y_ref):
    del be_ref
    xb = x_ref[...].astype(BF16)
    gate = jnp.dot(xb, wg_ref[0].astype(BF16), preferred_element_type=F32)
    up = jnp.dot(xb, wu_ref[0].astype(BF16), preferred_element_type=F32)
    hid = (gate * jax.nn.sigmoid(gate) * up).astype(BF16)
    y_ref[...] = jnp.dot(hid, wd_ref[0].astype(BF16), preferred_element_type=F32)


def _moe_experts(block_expert, xs, w_gate, w_up, w_down):
    n_slots, D = xs.shape
    n_blocks = n_slots // ROUTE_ROWS
    return pl.pallas_call(
        _moe_expert_kernel,
        out_shape=jax.ShapeDtypeStruct((n_slots, D), F32),
        grid_spec=pltpu.PrefetchScalarGridSpec(
            num_scalar_prefetch=1,
            grid=(n_blocks,),
            in_specs=[pl.BlockSpec((ROUTE_ROWS, D), lambda i, be: (i, 0)),
                      pl.BlockSpec((1, D, EXPERT_FF), lambda i, be: (be[i], 0, 0)),
                      pl.BlockSpec((1, D, EXPERT_FF), lambda i, be: (be[i], 0, 0)),
                      pl.BlockSpec((1, EXPERT_FF, D), lambda i, be: (be[i], 0, 0))],
            out_specs=pl.BlockSpec((ROUTE_ROWS, D), lambda i, be: (i, 0))),
        compiler_params=_cparams(("arbitrary",)),
        name="moe_experts",
    )(block_expert, xs, w_gate, w_up, w_down)


def _moe_combine_kernel(dest_ref, x1_ref, rt_ref, g2_ref, ys_ref, o_ref, buf, sem):
    def start(t, c):
        _row_copy(ys_ref, dest_ref[0, 0, 2 * t], buf.at[0], t, sem).start()
        _row_copy(ys_ref, dest_ref[0, 0, 2 * t + 1], buf.at[1], t, sem).start()
        return c

    def wait(t, c):
        _row_copy(ys_ref, dest_ref[0, 0, 2 * t], buf.at[0], t, sem).wait()
        _row_copy(ys_ref, dest_ref[0, 0, 2 * t + 1], buf.at[1], t, sem).wait()
        return c

    lax.fori_loop(0, TOK_BLOCK, start, 0)
    lax.fori_loop(0, TOK_BLOCK, wait, 0)
    rt = rt_ref[...]
    moe = rt[:, 2:3] * buf[0] + rt[:, 3:4] * buf[1]
    o_ref[...] = x1_ref[...] + g2_ref[0] * moe


def _moe_combine(dest3, x1, route, mod3, ys, per):
    N, D = x1.shape
    nb = dest3.shape[0]
    return pl.pallas_call(
        _moe_combine_kernel,
        out_shape=jax.ShapeDtypeStruct((N, D), F32),
        grid=(nb,),
        in_specs=[pl.BlockSpec((1, 1, 2 * TOK_BLOCK), lambda i: (i, 0, 0), memory_space=pltpu.SMEM),
                  pl.BlockSpec((TOK_BLOCK, D), lambda i: (i, 0)),
                  pl.BlockSpec((TOK_BLOCK, ROUTER_LANES), lambda i: (i, 0)),
                  pl.BlockSpec((1, 1, D), lambda i: (i // per, 0, 5)),
                  pl.BlockSpec(memory_space=pl.ANY)],
        out_specs=pl.BlockSpec((TOK_BLOCK, D), lambda i: (i, 0)),
        scratch_shapes=[pltpu.VMEM((2, TOK_BLOCK, D), F32), pltpu.SemaphoreType.DMA],
        compiler_params=_cparams(("arbitrary",)),
        name="moe_combine",
    )(dest3, x1, route, mod3, ys)


def _rope_tables(T):
    rows = T // GRID_W
    row_id = jnp.repeat(jnp.arange(rows), GRID_W).astype(F32)
    col_id = jnp.tile(jnp.arange(GRID_W), rows).astype(F32)
    inv = ROPE_THETA ** (-jnp.arange(0, AXIS_DIM, 2, dtype=F32) / AXIS_DIM)
    ar = row_id[:, None] * inv
    ac = col_id[:, None] * inv
    ang = jnp.concatenate([ar, ar, ac, ac], axis=-1)
    cos = jnp.concatenate([jnp.ones((TOK_BLOCK, HEAD_DIM), F32), jnp.cos(ang)], axis=0)
    sin = jnp.concatenate([jnp.zeros((TOK_BLOCK, HEAD_DIM), F32), jnp.sin(ang)], axis=0)
    return jnp.tile(cos, (1, 2)), jnp.tile(sin, (1, 2))


def _block_ones(n):
    g = jnp.arange(n) // HEAD_DIM
    return (g[:, None] == g[None, :]).astype(BF16)


def kernel(x, c, ctx, c_ctx, norm1_w, norm2_w, w_mod, b_mod, w_in, shift_w, rwkv_w0, rwkv_w_up,
           rwkv_a0, rwkv_a_up, rwkv_g_up, rwkv_k_k, rwkv_k_a, rwkv_r_k, rwkv_ln_w, rwkv_ln_b,
           q_norm_w, k_norm_w, lam_q1, lam_k1, lam_q2, lam_k2, subln_w, w_out, w_group, b_group,
           w_expert, b_expert, moe_w_gate, moe_w_up, moe_w_down):
    B, T, D = x.shape
    assert ctx.shape[1] == TOK_BLOCK and T % TOK_BLOCK == 0 and D == D_MODEL
    N = B * T

    mod_rows = (B + 1 + 7) // 8 * 8
    cc = jnp.zeros((mod_rows, D), F32).at[:B].set(c).at[B].set(c_ctx)
    mod = _modulation(cc, w_mod[0], b_mod[0][None])
    mod3 = mod.reshape(mod_rows, 1, N_MOD * D)
    zpad = jnp.zeros((2, 64, RWKV_WIDTH), F32)
    wup_pad = jnp.concatenate([rwkv_w_up[0], zpad], axis=1)
    aup_pad = jnp.concatenate([zpad, rwkv_a_up[0]], axis=1)
    row = lambda a: a.reshape(1, -1)
    qk_w = jnp.concatenate([jnp.tile(q_norm_w[0], DIFF_WIDTH // HEAD_DIM),
                            jnp.tile(k_norm_w[0], DIFF_WIDTH // HEAD_DIM)])[None]
    lam_p = jnp.stack([lam_q1[0], lam_k1[0], lam_q2[0], lam_k2[0]])
    rw = jnp.zeros((D, ROUTER_LANES), F32).at[:, :N_GROUPS].set(w_group[0])
    rw = rw.at[:, N_GROUPS:N_GROUPS + N_EXPERTS].set(w_expert[0])
    rw_hi = rw.astype(BF16)
    rw_lo = (rw - rw_hi.astype(F32)).astype(BF16)
    rb = jnp.zeros((1, ROUTER_LANES), F32).at[0, :N_GROUPS].set(b_group[0])
    rb = rb.at[0, N_GROUPS:N_GROUPS + N_EXPERTS].set(b_expert[0])
    cos, sin = _rope_tables(T)

    pr, pd = _inproj(ctx, x, mod3, row(norm1_w[0]), w_in[0].astype(BF16))

    r, v, kn, lw, kd, bb, g, bonus = _rwkv_prep(
        pr, shift_w[0], rwkv_w0[0][:, None, :], wup_pad, rwkv_a0[0][:, None, :], aup_pad,
        rwkv_g_up[0], row(rwkv_k_k[0]), row(rwkv_k_a[0]), row(rwkv_r_k[0]), _block_ones(RWKV_WIDTH))
    y_f = _rwkv_scan(0, r, v, kn, lw, kd, bb)
    y_b = _rwkv_scan(1, r, v, kn, lw, kd, bb)
    o_r = _rwkv_readout(y_f, y_b, g, bonus, row(rwkv_ln_w[0]), row(rwkv_ln_b[0]), _block_ones(LANES))

    qk = _qk_prep(pd, qk_w, cos, sin, _block_ones(2 * DIFF_WIDTH))
    o_d = _diff_attn(qk, pd, lam_p, row(subln_w[0]))

    wo = w_out[0].astype(BF16)
    x1, h2, route = _outproj(o_r, o_d, x, wo[:RWKV_WIDTH], wo[RWKV_WIDTH:], mod3,
                             row(norm2_w[0]), rw_hi, rw_lo, rb)

    R = ROUTE_ROWS
    flat_e = route.reshape(N, ROUTER_LANES)[:, :2].astype(jnp.int32).reshape(2 * N)
    onehot = (flat_e[:, None] == jnp.arange(N_EXPERTS, dtype=jnp.int32)[None, :]).astype(jnp.int32)
    csum = jnp.cumsum(onehot, axis=0)
    rank = jnp.take_along_axis(csum, flat_e[:, None], axis=1)[:, 0] - 1
    counts = csum[-1]
    padded = (counts + R - 1) // R * R
    pad_end = jnp.cumsum(padded)
    pad_start = pad_end - padded
    dest = (pad_start[flat_e] + rank).astype(jnp.int32)
    n_blocks = (2 * N + N_EXPERTS * (R - 1) + R - 1) // R
    block_expert = jnp.minimum(
        jnp.searchsorted(pad_end, jnp.arange(n_blocks, dtype=jnp.int32) * R, side='right'),
        N_EXPERTS - 1).astype(jnp.int32)
    dest3 = dest.reshape(N // TOK_BLOCK, 1, 2 * TOK_BLOCK)

    xs = _moe_scatter(dest3, h2.reshape(N, D), jnp.zeros((n_blocks * R, D), F32))
    ys = _moe_experts(block_expert, xs, moe_w_gate[0], moe_w_up[0], moe_w_down[0])
    out = _moe_combine(dest3, x1.reshape(N, D), route.reshape(N, ROUTER_LANES), mod3, ys,
                       T // TOK_BLOCK)
    return out.reshape(B, T, D)
```

```python
import functools
import math

import jax
import jax.numpy as jnp
from jax import lax
from jax.experimental import pallas as pl
from jax.experimental.pallas import tpu as pltpu

F32 = jnp.float32
BF16 = jnp.bfloat16

D_MODEL = 1024
HEAD_DIM = 64
RWKV_WIDTH = 512
RWKV_HEADS = 8
DIFF_WIDTH = 512
DIFF_HEADS = 4
DIR_LORA = 256
RWKV_IN = 2048
DIFF_IN = 1536
IN_WIDTH = RWKV_IN + DIFF_IN
GRID_W = 64
AXIS_DIM = HEAD_DIM // 2
ROPE_THETA = 10000.0
DIFF_SCALE = HEAD_DIM ** -0.5
N_GROUPS = 4
EXPERTS_PER_GROUP = 8
N_EXPERTS = 32
EXPERT_FF = 512
NORM_EPS = 1e-6
GN_EPS = 64e-5
N_MOD = 6
LAM_INIT = 0.8 - 0.6 * math.exp(-0.3 * 0)

TOK_BLOCK = 256
CHUNK = 64
CHUNKS_PER_BLOCK = TOK_BLOCK // CHUNK
HEAD_PAIRS = RWKV_HEADS // 2
SCAN_PAIRS = 4
LANES = 128
ROUTE_ROWS = 256
ROUTER_LANES = 128
ATTN_SUB = 64
LOG2E = math.log2(math.e)
VMEM_LIMIT = 56 * 1024 * 1024


def _cparams(sem):
    return pltpu.CompilerParams(dimension_semantics=sem, vmem_limit_bytes=VMEM_LIMIT)


def _mod_kernel(c_ref, w_ref, b_ref, o_ref):
    c = c_ref[...]
    s = c * jax.nn.sigmoid(c)
    o_ref[...] = jnp.dot(s, w_ref[...], preferred_element_type=F32,
                         precision=lax.Precision.HIGHEST) + b_ref[...]


def _modulation(cc, w_mod, b_mod):
    rows = cc.shape[0]
    ncol = w_mod.shape[1] // D_MODEL
    return pl.pallas_call(
        _mod_kernel,
        out_shape=jax.ShapeDtypeStruct((rows, w_mod.shape[1]), F32),
        grid=(ncol,),
        in_specs=[pl.BlockSpec((rows, D_MODEL), lambda j: (0, 0)),
                  pl.BlockSpec((D_MODEL, D_MODEL), lambda j: (0, j)),
                  pl.BlockSpec((1, D_MODEL), lambda j: (0, j))],
        out_specs=pl.BlockSpec((rows, D_MODEL), lambda j: (0, j)),
        compiler_params=_cparams(("arbitrary",)),
        name="modulation",
    )(cc, w_mod, b_mod)


def _inproj_kernel(ctx_ref, x_ref, sh_ref, sc_ref, nw_ref, w_ref, pr_ref, pd_ref):
    i = pl.program_id(1)
    xin = jnp.where(i == 0, ctx_ref[0], x_ref[0])
    ms = jnp.mean(xin * xin, axis=-1, keepdims=True)
    y = xin * lax.rsqrt(ms + NORM_EPS) * nw_ref[...]
    h = (y * (1.0 + sc_ref[0]) + sh_ref[0]).astype(BF16)
    p = jnp.dot(h, w_ref[...], preferred_element_type=F32)
    pr_ref[0] = p[:, :RWKV_IN].astype(BF16)
    pd_ref[0] = p[:, RWKV_IN:].astype(BF16)


def _mod_row(nb):
    return lambda b, i: jnp.where(i == 0, nb, b)


def _inproj(ctx, x, mod3, norm1_w, w_in_bf):
    B, T, D = x.shape
    nblk = 1 + T // TOK_BLOCK
    S = nblk * TOK_BLOCK
    row = _mod_row(B)
    return pl.pallas_call(
        _inproj_kernel,
        out_shape=(jax.ShapeDtypeStruct((B, S, RWKV_IN), BF16),
                   jax.ShapeDtypeStruct((B, S, DIFF_IN), BF16)),
        grid=(B, nblk),
        in_specs=[pl.BlockSpec((1, TOK_BLOCK, D), lambda b, i: (b, 0, 0)),
                  pl.BlockSpec((1, TOK_BLOCK, D), lambda b, i: (b, jnp.maximum(i - 1, 0), 0)),
                  pl.BlockSpec((1, 1, D), lambda b, i: (row(b, i), 0, 0)),
                  pl.BlockSpec((1, 1, D), lambda b, i: (row(b, i), 0, 1)),
                  pl.BlockSpec((1, D), lambda b, i: (0, 0)),
                  pl.BlockSpec((D, IN_WIDTH), lambda b, i: (0, 0))],
        out_specs=(pl.BlockSpec((1, TOK_BLOCK, RWKV_IN), lambda b, i: (b, i, 0)),
                   pl.BlockSpec((1, TOK_BLOCK, DIFF_IN), lambda b, i: (b, i, 0))),
        compiler_params=_cparams(("parallel", "arbitrary")),
        name="inproj",
    )(ctx, x, mod3, mod3, norm1_w, w_in_bf)


def _head_sum(v, e_ref):
    return jnp.dot(v, e_ref[...], preferred_element_type=F32)


def _rwkv_prep_kernel(nblk, z_ref, zp_ref, zn_ref, sw_ref, w0_ref, wup_ref, a0_ref, aup_ref,
                      gup_ref, kk_ref, ka_ref, rk_ref, e_ref,
                      r_ref, v_ref, kn_ref, lw_ref, kd_ref, bb_ref, g_ref, bo_ref):
    i = pl.program_id(1)
    z = z_ref[0].astype(F32)
    row = lax.broadcasted_iota(jnp.int32, (TOK_BLOCK, 1), 0)
    prev_ok = jnp.logical_and(i != 0, i != 1)
    next_ok = jnp.logical_and(i != 0, i != nblk - 1)
    zp = zp_ref[0][15:16, :].astype(F32) * prev_ok.astype(F32)
    zn = zn_ref[0][0:1, :].astype(F32) * next_ok.astype(F32)
    prev = jnp.where(row == 0, zp, pltpu.roll(z, 1, 0))
    nxt = jnp.where(row == TOK_BLOCK - 1, zn, pltpu.roll(z, TOK_BLOCK - 1, 0))
    sw = sw_ref[...]
    rx = sw[0:1] * prev + sw[1:2] * z + sw[2:3] * nxt
    r = rx[:, 0:512]
    k = rx[:, 512:1024]
    v = rx[:, 1024:1536]
    kkf = k * kk_ref[...]
    nrm = jnp.sqrt(_head_sum(kkf * kkf, e_ref))
    kn = kkf / jnp.maximum(nrm, 1e-12)
    for p in range(HEAD_PAIRS):
        sl = slice(p * LANES, (p + 1) * LANES)
        r_ref[0, p] = r[:, sl].astype(BF16)
        v_ref[0, p] = v[:, sl].astype(BF16)
        kn_ref[0, p] = kn[:, sl].astype(BF16)
    for d in range(2):
        base = 1536 + d * DIR_LORA
        x128 = rx[:, base:base + 128]
        lg = rx[:, base + 128:base + 256]
        w_raw = w0_ref[d] + jnp.dot(jnp.tanh(x128), wup_ref[d], preferred_element_type=F32)
        u = -w_raw
        sp = jnp.maximum(u, 0.0) + jnp.log(1.0 + jnp.exp(-jnp.abs(u)))
        logw = -jnp.exp(-sp - 0.5)
        a = jax.nn.sigmoid(a0_ref[d] + jnp.dot(x128, aup_ref[d], preferred_element_type=F32))
        g = jnp.dot(jax.nn.sigmoid(lg), gup_ref[d], preferred_element_type=F32)
        kd = k * (1.0 + (a - 1.0) * ka_ref[...])
        bonus = _head_sum(r * kd * rk_ref[...], e_ref) * v
        bb = kn * a
        for p in range(HEAD_PAIRS):
            sl = slice(p * LANES, (p + 1) * LANES)
            lw_ref[d, 0, p] = logw[:, sl]
            kd_ref[d, 0, p] = kd[:, sl].astype(BF16)
            bb_ref[d, 0, p] = bb[:, sl].astype(BF16)
        g_ref[d, 0] = g.astype(BF16)
        bo_ref[d, 0] = bonus.astype(BF16)


def _rwkv_prep(pr, shift_w, w0, wup_pad, a0, aup_pad, g_up, k_k, k_a, r_k, e512):
    B, S, _ = pr.shape
    nblk = S // TOK_BLOCK
    hb = TOK_BLOCK // 16
    nh = S // 16
    pair = lambda dt: jax.ShapeDtypeStruct((B, HEAD_PAIRS, S, LANES), dt)
    pair2 = lambda dt: jax.ShapeDtypeStruct((2, B, HEAD_PAIRS, S, LANES), dt)
    chan2 = jax.ShapeDtypeStruct((2, B, S, RWKV_WIDTH), BF16)
    pspec = pl.BlockSpec((1, HEAD_PAIRS, TOK_BLOCK, LANES), lambda b, i: (b, 0, i, 0))
    pspec2 = pl.BlockSpec((2, 1, HEAD_PAIRS, TOK_BLOCK, LANES), lambda b, i: (0, b, 0, i, 0))
    cspec2 = pl.BlockSpec((2, 1, TOK_BLOCK, RWKV_WIDTH), lambda b, i: (0, b, i, 0))
    full = lambda a: pl.BlockSpec(a.shape, lambda b, i: (0,) * a.ndim)
    return pl.pallas_call(
        functools.partial(_rwkv_prep_kernel, nblk),
        out_shape=(pair(BF16), pair(BF16), pair(BF16), pair2(F32), pair2(BF16), pair2(BF16),
                   chan2, chan2),
        grid=(B, nblk),
        in_specs=[pl.BlockSpec((1, TOK_BLOCK, RWKV_IN), lambda b, i: (b, i, 0)),
                  pl.BlockSpec((1, 16, RWKV_IN), lambda b, i: (b, jnp.maximum(i * hb - 1, 0), 0)),
                  pl.BlockSpec((1, 16, RWKV_IN), lambda b, i: (b, jnp.minimum((i + 1) * hb, nh - 1), 0)),
                  full(shift_w), full(w0), full(wup_pad), full(a0), full(aup_pad), full(g_up),
                  full(k_k), full(k_a), full(r_k), full(e512)],
        out_specs=(pspec, pspec, pspec, pspec2, pspec2, pspec2, cspec2, cspec2),
        compiler_params=_cparams(("parallel", "arbitrary")),
        name="rwkv_prep",
    )(pr, pr, pr, shift_w, w0, wup_pad, a0, aup_pad, g_up, k_k, k_a, r_k, e512)


def _bmm(a, b):
    return jnp.einsum('nij,njk->nik', a.astype(BF16), b.astype(BF16), preferred_element_type=F32)


def _bmm_nt(a, b):
    return jnp.einsum('nij,nkj->nik', a.astype(BF16), b.astype(BF16), preferred_element_type=F32)


def _mm(a, b):
    return jnp.dot(a.astype(BF16), b.astype(BF16), preferred_element_type=F32)


def _rwkv_scan_kernel(reverse, r_ref, v_ref, kn_ref, lw_ref, kd_ref, bb_ref, y_ref,
                      h_ref, q_s, y0_s, m_s, n_s, pt_s):
    C = CHUNK
    NC = CHUNKS_PER_BLOCK

    @pl.when(pl.program_id(1) == 0)
    def _():
        h_ref[...] = jnp.zeros_like(h_ref)

    t_i = lax.broadcasted_iota(jnp.int32, (C, C), 0)
    s_i = lax.broadcasted_iota(jnp.int32, (C, C), 1)
    if reverse:
        strict = s_i > t_i
        incl = s_i >= t_i
    else:
        strict = s_i < t_i
        incl = s_i <= t_i
    blk16 = (t_i // 16) == (s_i // 16)
    blk32 = (t_i // 32) == (s_i // 32)
    eye = (t_i == s_i).astype(F32)
    G = SCAN_PAIRS
    NB = G * NC
    tri = jnp.broadcast_to(incl.astype(F32)[None], (NB, C, C))
    lane = lax.broadcasted_iota(jnp.int32, (1, 1, LANES), 2)
    head0 = lane < HEAD_DIM
    rr = lax.broadcasted_iota(jnp.int32, (LANES, LANES), 0)
    cc = lax.broadcasted_iota(jnp.int32, (LANES, LANES), 1)
    same_head = ((rr // HEAD_DIM) == (cc // HEAD_DIM))[None]

    def chunked(blk):
        return blk.astype(F32).reshape(NB, C, LANES)

    def group_body(gi, carry):
        ps = pl.ds(gi * G, G)
        r = chunked(r_ref[0, ps])
        v = chunked(v_ref[0, ps])
        kn = chunked(kn_ref[0, ps])
        kd = chunked(kd_ref[0, 0, ps])
        bb = chunked(bb_ref[0, 0, ps])
        logw = lw_ref[0, 0, ps].reshape(NB, C, LANES)
        lw_hi = logw.astype(BF16).astype(F32)
        lw_lo = logw - lw_hi
        L = _bmm(tri, lw_hi) + _bmm(tri, lw_lo)
        Lex = L - logw
        tot = L[:, 0:1, :] if reverse else L[:, C - 1:C, :]
        e_in = jnp.exp(-L)
        e_hat = jnp.exp(tot - L)
        at = -kn * jnp.exp(Lex)
        rt = r * jnp.exp(L)
        bt = bb * e_in
        kt = kd * e_in
        bh = bb * e_hat
        kh = kd * e_hat
        lhs = jnp.concatenate([at, rt], axis=1)
        rhs = jnp.concatenate([bt, kt], axis=1)
        sc = jnp.concatenate([_bmm_nt(jnp.where(head0, lhs, 0.0), rhs),
                              _bmm_nt(jnp.where(head0, 0.0, lhs), rhs)], axis=0)
        a_ab = jnp.where(strict, sc[:, :C, :C], 0.0)
        a_ak = jnp.where(strict, sc[:, :C, C:], 0.0)
        a_rb = jnp.where(incl, sc[:, C:, :C], 0.0)
        a_rk = jnp.where(incl, sc[:, C:, C:], 0.0)
        nd = jnp.where(blk16, a_ab, 0.0)
        T = eye + nd
        pw = _bmm(nd, nd)
        T = T + _bmm(pw, T)
        pw = _bmm(pw, pw)
        T = T + _bmm(pw, T)
        pw = _bmm(pw, pw)
        T = T + _bmm(pw, T)
        off = jnp.where(jnp.logical_and(blk32, jnp.logical_not(blk16)), a_ab, 0.0)
        T = T + _bmm(T, _bmm(off, T))
        off = jnp.where(blk32, 0.0, a_ab)
        T = T + _bmm(T, _bmm(off, T))
        v2 = jnp.concatenate([v, v], axis=0)
        at2 = jnp.concatenate([at, at], axis=0)
        akv = _bmm(a_ak, v2)
        arkv = _bmm(a_rk, v2)
        wu = _bmm(T, jnp.concatenate([at2, akv], axis=2))
        qy = _bmm(a_rb, wu)
        merge = lambda zz: jnp.where(head0, zz[:NB], zz[NB:])
        wup = jnp.concatenate([merge(wu[:, :, :LANES]), merge(wu[:, :, LANES:])], axis=2)
        q_s[...] = (rt + merge(qy[:, :, :LANES])).reshape(G, NC, C, LANES)
        y0_s[...] = merge(qy[:, :, LANES:] + arkv).reshape(G, NC, C, LANES)
        bht = jnp.swapaxes(bh, 1, 2)
        kht = jnp.swapaxes(kh, 1, 2)
        mn = _bmm(bht, wup)
        m_s[...] = jnp.where(same_head, mn[:, :, :LANES], 0.0).reshape(G, NC, LANES, LANES)
        n_s[...] = jnp.where(same_head, mn[:, :, LANES:] + _bmm(kht, v), 0.0).reshape(G, NC, LANES, LANES)
        pt_s[...] = jnp.exp(jnp.sum(jnp.swapaxes(logw, 1, 2), axis=2, keepdims=True)).reshape(G, NC, LANES, 1)
        for j in range(NC):
            c = NC - 1 - j if reverse else j
            H = h_ref[ps]
            y_ref[0, ps, c * C:(c + 1) * C, :] = _bmm(q_s[:, c], H) + y0_s[:, c]
            h_ref[ps] = pt_s[:, c] * H + _bmm(m_s[:, c], H) + n_s[:, c]
        return carry

    if G == HEAD_PAIRS:
        group_body(0, 0)
    else:
        lax.fori_loop(0, HEAD_PAIRS // G, group_body, 0)


def _rwkv_scan(d, r, v, kn, lw, kd, bb):
    B, _, S, _ = r.shape
    nblk = S // TOK_BLOCK
    reverse = d == 1
    if reverse:
        blk = lambda i: jnp.where(i == 0, 0, nblk - i)
    else:
        blk = lambda i: i
    pspec = pl.BlockSpec((1, HEAD_PAIRS, TOK_BLOCK, LANES), lambda b, i: (b, 0, blk(i), 0))
    pspec_d = pl.BlockSpec((1, 1, HEAD_PAIRS, TOK_BLOCK, LANES), lambda b, i: (d, b, 0, blk(i), 0))
    NC, C, G = CHUNKS_PER_BLOCK, CHUNK, SCAN_PAIRS
    return pl.pallas_call(
        functools.partial(_rwkv_scan_kernel, reverse),
        out_shape=jax.ShapeDtypeStruct((B, HEAD_PAIRS, S, LANES), F32),
        grid=(B, nblk),
        in_specs=[pspec, pspec, pspec, pspec_d, pspec_d, pspec_d],
        out_specs=pspec,
        scratch_shapes=[pltpu.VMEM((HEAD_PAIRS, LANES, LANES), F32),
                        pltpu.VMEM((G, NC, C, LANES), F32),
                        pltpu.VMEM((G, NC, C, LANES), F32),
                        pltpu.VMEM((G, NC, LANES, LANES), F32),
                        pltpu.VMEM((G, NC, LANES, LANES), F32),
                        pltpu.VMEM((G, NC, LANES, 1), F32)],
        compiler_params=_cparams(("parallel", "arbitrary")),
        name="rwkv_scan_rev" if reverse else "rwkv_scan_fwd",
    )(r, v, kn, lw, kd, bb)


def _rwkv_readout_kernel(y0_ref, y1_ref, g_ref, bo_ref, lnw_ref, lnb_ref, e_ref, o_ref):
    inv = 1.0 / HEAD_DIM
    for p in range(HEAD_PAIRS):
        sl = slice(p * LANES, (p + 1) * LANES)
        acc = None
        for d, y_ref in enumerate((y0_ref, y1_ref)):
            y = y_ref[0, p]
            mu = _head_sum(y, e_ref) * inv
            dl = y - mu
            var = _head_sum(dl * dl, e_ref) * inv
            yn = dl * lax.rsqrt(var + GN_EPS) * lnw_ref[:, sl] + lnb_ref[:, sl]
            o = (yn + bo_ref[d, 0, :, sl].astype(F32)) * g_ref[d, 0, :, sl].astype(F32)
            acc = o if acc is None else acc + o
        o_ref[0, :, sl] = acc.astype(BF16)


def _rwkv_readout(y0, y1, g, bonus, ln_w, ln_b, e128):
    B, _, S, _ = y0.shape
    T = S - TOK_BLOCK
    yspec = pl.BlockSpec((1, HEAD_PAIRS, TOK_BLOCK, LANES), lambda b, i: (b, 0, i + 1, 0))
    cspec2 = pl.BlockSpec((2, 1, TOK_BLOCK, RWKV_WIDTH), lambda b, i: (0, b, i + 1, 0))
    full = lambda a: pl.BlockSpec(a.shape, lambda b, i: (0,) * a.ndim)
    return pl.pallas_call(
        _rwkv_readout_kernel,
        out_shape=jax.ShapeDtypeStruct((B, T, RWKV_WIDTH), BF16),
        grid=(B, T // TOK_BLOCK),
        in_specs=[yspec, yspec, cspec2, cspec2, full(ln_w), full(ln_b), full(e128)],
        out_specs=pl.BlockSpec((1, TOK_BLOCK, RWKV_WIDTH), lambda b, i: (b, i, 0)),
        compiler_params=_cparams(("parallel", "arbitrary")),
        name="rwkv_readout",
    )(y0, y1, g, bonus, ln_w, ln_b, e128)


def _qk_prep_kernel(z_ref, nw_ref, cos_ref, sin_ref, e_ref, o_ref):
    z = z_ref[0].astype(F32)
    ms = _head_sum(z * z, e_ref) * (1.0 / HEAD_DIM)
    zn = z * lax.rsqrt(ms + NORM_EPS) * nw_ref[...]
    cos = cos_ref[...]
    sin = sin_ref[...]
    lane = lax.broadcasted_iota(jnp.int32, (1, LANES), 1)
    first = (lane % (2 * 16)) < 16
    for j in range(2 * DIFF_WIDTH // LANES):
        sl = slice(j * LANES, (j + 1) * LANES)
        t = zn[:, sl]
        rot = jnp.where(first, -pltpu.roll(t, LANES - 16, 1), pltpu.roll(t, 16, 1))
        out = t * cos + rot * sin
        if j < DIFF_WIDTH // LANES:
            out = out * (DIFF_SCALE * LOG2E)
        o_ref[0, :, sl] = out.astype(BF16)


def _qk_prep(pd, qk_w, cos, sin, e1024):
    B, S, _ = pd.shape
    full = lambda a: pl.BlockSpec(a.shape, lambda b, i: (0,) * a.ndim)
    return pl.pallas_call(
        _qk_prep_kernel,
        out_shape=jax.ShapeDtypeStruct((B, S, 2 * DIFF_WIDTH), BF16),
        grid=(B, S // TOK_BLOCK),
        in_specs=[pl.BlockSpec((1, TOK_BLOCK, 2 * DIFF_WIDTH), lambda b, i: (b, i, 0)),
                  full(qk_w),
                  pl.BlockSpec((TOK_BLOCK, LANES), lambda b, i: (i, 0)),
                  pl.BlockSpec((TOK_BLOCK, LANES), lambda b, i: (i, 0)),
                  full(e1024)],
        out_specs=pl.BlockSpec((1, TOK_BLOCK, 2 * DIFF_WIDTH), lambda b, i: (b, i, 0)),
        compiler_params=_cparams(("parallel", "arbitrary")),
        name="qk_prep",
    )(pd, qk_w, cos, sin, e1024)


def _diff_attn_kernel(nkv, q_ref, k_ref, v_ref, lam_ref, sw_ref, o_ref,
                      qs_ref, s0_ref, s1_ref, p0_ref, p1_ref, al0_ref, al1_ref, m_ref, l_ref, acc_ref):
    TQ = TOK_BLOCK
    TK = TOK_BLOCK
    q = q_ref[0]
    lane = lax.broadcasted_iota(jnp.int32, (1, LANES), 1)
    zero = jnp.zeros_like(q)
    qs_ref[0:TQ, :] = jnp.where(lane < HEAD_DIM, q, zero)
    qs_ref[TQ:2 * TQ, :] = jnp.where(lane < HEAD_DIM, zero, q)
    m_ref[...] = jnp.full(m_ref.shape, -1e30, F32)
    l_ref[...] = jnp.zeros(l_ref.shape, F32)
    acc_ref[...] = jnp.zeros(acc_ref.shape, F32)
    p1_ref[...] = jnp.zeros(p1_ref.shape, BF16)
    al1_ref[...] = jnp.ones(al1_ref.shape, F32)

    def chunk(ref, j):
        return ref[0, pl.ds(pl.multiple_of(j * TK, TK), TK), :]

    def scores(j, s_ref):
        s_ref[...] = lax.dot_general(qs_ref[...], chunk(k_ref, j), (((1,), (1,)), ((), ())),
                                     preferred_element_type=F32)

    def softmax_step(s_ref, p_ref, al_ref):
        for r in range(2 * TQ // ATTN_SUB):
            rows = slice(r * ATTN_SUB, (r + 1) * ATTN_SUB)
            s = s_ref[rows, :]
            m_old = m_ref[rows, :]
            mx = jnp.maximum(s[:, :LANES], s[:, LANES:])
            m_new = jnp.maximum(m_old, jnp.max(mx, axis=1, keepdims=True))
            al = jnp.exp2(m_old - m_new)
            p = jnp.exp2(s - jnp.concatenate([m_new, m_new], axis=1))
            l_ref[rows, :] = al * l_ref[rows, :] + (p[:, :LANES] + p[:, LANES:])
            m_ref[rows, :] = m_new
            al_ref[rows, :] = al
            p_ref[rows, :] = p.astype(BF16)

    def values(j, p_ref, al_ref):
        acc_ref[...] = al_ref[...] * acc_ref[...] + jnp.dot(p_ref[...], chunk(v_ref, j),
                                                             preferred_element_type=F32)

    scores(0, s0_ref)

    def pair(t, carry):
        j = 2 * t
        scores(j + 1, s1_ref)
        softmax_step(s0_ref, p0_ref, al0_ref)
        values(jnp.maximum(j - 1, 0), p1_ref, al1_ref)
        scores(jnp.minimum(j + 2, nkv - 1), s0_ref)
        softmax_step(s1_ref, p1_ref, al1_ref)
        values(j, p0_ref, al0_ref)
        return carry

    lax.fori_loop(0, nkv // 2, pair, 0)
    if nkv % 2:
        softmax_step(s0_ref, p0_ref, al0_ref)
        if nkv > 1:
            values(nkv - 2, p1_ref, al1_ref)
        values(nkv - 1, p0_ref, al0_ref)
    else:
        values(nkv - 1, p1_ref, al1_ref)

    o = acc_ref[...] / jnp.sum(l_ref[...], axis=1, keepdims=True)
    lp = lam_ref[...]
    lam = (jnp.exp(jnp.sum(lp[0:1] * lp[1:2], axis=1, keepdims=True))
           - jnp.exp(jnp.sum(lp[2:3] * lp[3:4], axis=1, keepdims=True)) + LAM_INIT)
    od = o[:TQ] - lam * o[TQ:]
    ms = jnp.mean(od * od, axis=-1, keepdims=True)
    o_ref[0] = (od * lax.rsqrt(ms + NORM_EPS) * sw_ref[...] * (1.0 - LAM_INIT)).astype(BF16)


def _diff_attn(qk, pd, lam_p, subln_w):
    B, S, _ = qk.shape
    T = S - TOK_BLOCK
    H = DIFF_HEADS
    TQ2 = 2 * TOK_BLOCK
    return pl.pallas_call(
        functools.partial(_diff_attn_kernel, S // TOK_BLOCK),
        out_shape=jax.ShapeDtypeStruct((B, T, DIFF_WIDTH), BF16),
        grid=(B, H, T // TOK_BLOCK),
        in_specs=[pl.BlockSpec((1, TOK_BLOCK, LANES), lambda b, h, i: (b, i + 1, h)),
                  pl.BlockSpec((1, S, LANES), lambda b, h, i: (b, 0, H + h)),
                  pl.BlockSpec((1, S, LANES), lambda b, h, i: (b, 0, 2 * H + h)),
                  pl.BlockSpec(lam_p.shape, lambda b, h, i: (0, 0)),
                  pl.BlockSpec(subln_w.shape, lambda b, h, i: (0, 0))],
        out_specs=pl.BlockSpec((1, TOK_BLOCK, LANES), lambda b, h, i: (b, i, h)),
        scratch_shapes=[pltpu.VMEM((TQ2, LANES), BF16),
                        pltpu.VMEM((TQ2, 2 * LANES), F32), pltpu.VMEM((TQ2, 2 * LANES), F32),
                        pltpu.VMEM((TQ2, 2 * LANES), BF16), pltpu.VMEM((TQ2, 2 * LANES), BF16),
                        pltpu.VMEM((TQ2, LANES), F32), pltpu.VMEM((TQ2, LANES), F32),
                        pltpu.VMEM((TQ2, LANES), F32), pltpu.VMEM((TQ2, LANES), F32),
                        pltpu.VMEM((TQ2, LANES), F32)],
        compiler_params=_cparams(("parallel", "parallel", "arbitrary")),
        name="diff_attn",
    )(qk, qk, pd, lam_p, subln_w)


def _outproj_kernel(or_ref, od_ref, x_ref, wr_ref, wd_ref, g1_ref, sh_ref, sc_ref, nw_ref,
                    rwh_ref, rwl_ref, rb_ref, x1_ref, h2_ref, rt_ref):
    attn = (jnp.dot(or_ref[0], wr_ref[...], preferred_element_type=F32)
            + jnp.dot(od_ref[0], wd_ref[...], preferred_element_type=F32))
    x1 = x_ref[0] + g1_ref[0] * attn
    x1_ref[0] = x1
    ms = jnp.mean(x1 * x1, axis=-1, keepdims=True)
    h2 = x1 * lax.rsqrt(ms + NORM_EPS) * nw_ref[...] * (1.0 + sc_ref[0]) + sh_ref[0]
    h2_ref[0] = h2
    hh = h2.astype(BF16)
    hl = (h2 - hh.astype(F32)).astype(BF16)
    logits = (jnp.dot(hh, rwh_ref[...], preferred_element_type=F32)
              + jnp.dot(hl, rwh_ref[...], preferred_element_type=F32)
              + jnp.dot(hh, rwl_ref[...], preferred_element_type=F32)) + rb_ref[...]
    lane = lax.broadcasted_iota(jnp.int32, logits.shape, 1).astype(F32)
    neg = -jnp.inf
    big = 1e9
    gl = jnp.where(lane < N_GROUPS, logits, neg)
    gmax = jnp.max(gl, axis=1, keepdims=True)
    gsel = jnp.min(jnp.where(gl == gmax, lane, big), axis=1, keepdims=True)
    gprob = 1.0 / jnp.sum(jnp.exp(gl - gmax), axis=1, keepdims=True)
    lo = N_GROUPS + EXPERTS_PER_GROUP * gsel
    el = jnp.where(jnp.logical_and(lane >= lo, lane < lo + EXPERTS_PER_GROUP), logits, neg)
    v1 = jnp.max(el, axis=1, keepdims=True)
    i1 = jnp.min(jnp.where(el == v1, lane, big), axis=1, keepdims=True)
    el2 = jnp.where(lane == i1, neg, el)
    v2 = jnp.max(el2, axis=1, keepdims=True)
    i2 = jnp.min(jnp.where(el2 == v2, lane, big), axis=1, keepdims=True)
    e21 = jnp.exp(v2 - v1)
    w1 = gprob / (1.0 + e21)
    w2 = gprob * e21 / (1.0 + e21)
    rt_ref[0] = jnp.where(lane == 0, i1 - N_GROUPS,
                          jnp.where(lane == 1, i2 - N_GROUPS,
                                    jnp.where(lane == 2, w1, jnp.where(lane == 3, w2, 0.0))))


def _outproj(o_r, o_d, x, w_out_r, w_out_d, mod3, norm2_w, rw_hi, rw_lo, rb):
    B, T, D = x.shape
    full = lambda a: pl.BlockSpec(a.shape, lambda b, i: (0,) * a.ndim)
    modspec = lambda col: pl.BlockSpec((1, 1, D), lambda b, i: (b, 0, col))
    tok = lambda w: pl.BlockSpec((1, TOK_BLOCK, w), lambda b, i: (b, i, 0))
    return pl.pallas_call(
        _outproj_kernel,
        out_shape=(jax.ShapeDtypeStruct((B, T, D), F32),
                   jax.ShapeDtypeStruct((B, T, D), F32),
                   jax.ShapeDtypeStruct((B, T, ROUTER_LANES), F32)),
        grid=(B, T // TOK_BLOCK),
        in_specs=[tok(RWKV_WIDTH), tok(DIFF_WIDTH), tok(D), full(w_out_r), full(w_out_d),
                  modspec(2), modspec(3), modspec(4), full(norm2_w),
                  full(rw_hi), full(rw_lo), full(rb)],
        out_specs=(tok(D), tok(D), tok(ROUTER_LANES)),
        compiler_params=_cparams(("parallel", "arbitrary")),
        name="outproj_router",
    )(o_r, o_d, x, w_out_r, w_out_d, mod3, mod3, mod3, norm2_w, rw_hi, rw_lo, rb)


def _row_copy(src_ref, src_row, dst_ref, dst_row, sem):
    return pltpu.make_async_copy(src_ref.at[pl.ds(src_row, 1), :], dst_ref.at[pl.ds(dst_row, 1), :], sem)


def _moe_scatter_kernel(dest_ref, h_ref, xs_in_ref, xs_ref, sem):
    del xs_in_ref

    def start(t, c):
        _row_copy(h_ref, t, xs_ref, dest_ref[0, 0, 2 * t], sem).start()
        _row_copy(h_ref, t, xs_ref, dest_ref[0, 0, 2 * t + 1], sem).start()
        return c

    def wait(t, c):
        _row_copy(h_ref, t, xs_ref, dest_ref[0, 0, 2 * t], sem).wait()
        _row_copy(h_ref, t, xs_ref, dest_ref[0, 0, 2 * t + 1], sem).wait()
        return c

    lax.fori_loop(0, TOK_BLOCK, start, 0, unroll=8)
    lax.fori_loop(0, TOK_BLOCK, wait, 0, unroll=8)


def _moe_scatter(dest3, h2, xs0):
    N, D = h2.shape
    return pl.pallas_call(
        _moe_scatter_kernel,
        out_shape=jax.ShapeDtypeStruct(xs0.shape, xs0.dtype),
        grid=(N // TOK_BLOCK,),
        in_specs=[pl.BlockSpec((1, 1, 2 * TOK_BLOCK), lambda i: (i, 0, 0), memory_space=pltpu.SMEM),
                  pl.BlockSpec((TOK_BLOCK, D), lambda i: (i, 0)),
                  pl.BlockSpec(memory_space=pl.ANY)],
        out_specs=pl.BlockSpec(memory_space=pl.ANY),
        scratch_shapes=[pltpu.SemaphoreType.DMA],
        input_output_aliases={2: 0},
        compiler_params=_cparams(("arbitrary",)),
        name="moe_scatter",
    )(dest3, h2, xs0)


def _moe_expert_kernel(be_ref, x_ref, wg_ref, wu_ref, wd_ref, y_ref):
    del be_ref
    xb = x_ref[...].astype(BF16)
    gate = jnp.dot(xb, wg_ref[0].astype(BF16), preferred_element_type=F32)
    up = jnp.dot(xb, wu_ref[0].astype(BF16), preferred_element_type=F32)
    hid = (gate * jax.nn.sigmoid(gate) * up).astype(BF16)
    y_ref[...] = jnp.dot(hid, wd_ref[0].astype(BF16), preferred_element_type=F32)


def _moe_experts(block_expert, xs, w_gate, w_up, w_down):
    n_slots, D = xs.shape
    n_blocks = n_slots // ROUTE_ROWS
    return pl.pallas_call(
        _moe_expert_kernel,
        out_shape=jax.ShapeDtypeStruct((n_slots, D), F32),
        grid_spec=pltpu.PrefetchScalarGridSpec(
            num_scalar_prefetch=1,
            grid=(n_blocks,),
            in_specs=[pl.BlockSpec((ROUTE_ROWS, D), lambda i, be: (i, 0)),
                      pl.BlockSpec((1, D, EXPERT_FF), lambda i, be: (be[i], 0, 0)),
                      pl.BlockSpec((1, D, EXPERT_FF), lambda i, be: (be[i], 0, 0)),
                      pl.BlockSpec((1, EXPERT_FF, D), lambda i, be: (be[i], 0, 0))],
            out_specs=pl.BlockSpec((ROUTE_ROWS, D), lambda i, be: (i, 0))),
        compiler_params=_cparams(("arbitrary",)),
        name="moe_experts",
    )(block_expert, xs, w_gate, w_up, w_down)


def _moe_combine_kernel(dest_ref, x1_ref, rt_ref, g2_ref, ys_ref, o_ref, buf, sem):
    def start(t, c):
        _row_copy(ys_ref, dest_ref[0, 0, 2 * t], buf.at[0], t, sem).start()
        _row_copy(ys_ref, dest_ref[0, 0, 2 * t + 1], buf.at[1], t, sem).start()
        return c

    def wait(t, c):
        _row_copy(ys_ref, dest_ref[0, 0, 2 * t], buf.at[0], t, sem).wait()
        _row_copy(ys_ref, dest_ref[0, 0, 2 * t + 1], buf.at[1], t, sem).wait()
        return c

    lax.fori_loop(0, TOK_BLOCK, start, 0, unroll=8)
    lax.fori_loop(0, TOK_BLOCK, wait, 0, unroll=8)
    rt = rt_ref[...]
    moe = rt[:, 2:3] * buf[0] + rt[:, 3:4] * buf[1]
    o_ref[...] = x1_ref[...] + g2_ref[0] * moe


def _moe_combine(dest3, x1, route, mod3, ys, per):
    N, D = x1.shape
    nb = dest3.shape[0]
    return pl.pallas_call(
        _moe_combine_kernel,
        out_shape=jax.ShapeDtypeStruct((N, D), F32),
        grid=(nb,),
        in_specs=[pl.BlockSpec((1, 1, 2 * TOK_BLOCK), lambda i: (i, 0, 0), memory_space=pltpu.SMEM),
                  pl.BlockSpec((TOK_BLOCK, D), lambda i: (i, 0)),
                  pl.BlockSpec((TOK_BLOCK, ROUTER_LANES), lambda i: (i, 0)),
                  pl.BlockSpec((1, 1, D), lambda i: (i // per, 0, 5)),
                  pl.BlockSpec(memory_space=pl.ANY)],
        out_specs=pl.BlockSpec((TOK_BLOCK, D), lambda i: (i, 0)),
        scratch_shapes=[pltpu.VMEM((2, TOK_BLOCK, D), F32), pltpu.SemaphoreType.DMA],
        compiler_params=_cparams(("arbitrary",)),
        name="moe_combine",
    )(dest3, x1, route, mod3, ys)


def _rope_tables(T):
    rows = T // GRID_W
    row_id = jnp.repeat(jnp.arange(rows), GRID_W).astype(F32)
    col_id = jnp.tile(jnp.arange(GRID_W), rows).astype(F32)
    inv = ROPE_THETA ** (-jnp.arange(0, AXIS_DIM, 2, dtype=F32) / AXIS_DIM)
    ar = row_id[:, None] * inv
    ac = col_id[:, None] * inv
    ang = jnp.concatenate([ar, ar, ac, ac], axis=-1)
    cos = jnp.concatenate([jnp.ones((TOK_BLOCK, HEAD_DIM), F32), jnp.cos(ang)], axis=0)
    sin = jnp.concatenate([jnp.zeros((TOK_BLOCK, HEAD_DIM), F32), jnp.sin(ang)], axis=0)
    return jnp.tile(cos, (1, 2)), jnp.tile(sin, (1, 2))


def _block_ones(n):
    g = jnp.arange(n) // HEAD_DIM
    return (g[:, None] == g[None, :]).astype(BF16)


def kernel(x, c, ctx, c_ctx, norm1_w, norm2_w, w_mod, b_mod, w_in, shift_w, rwkv_w0, rwkv_w_up,
           rwkv_a0, rwkv_a_up, rwkv_g_up, rwkv_k_k, rwkv_k_a, rwkv_r_k, rwkv_ln_w, rwkv_ln_b,
           q_norm_w, k_norm_w, lam_q1, lam_k1, lam_q2, lam_k2, subln_w, w_out, w_group, b_group,
           w_expert, b_expert, moe_w_gate, moe_w_up, moe_w_down):
    B, T, D = x.shape
    assert ctx.shape[1] == TOK_BLOCK and T % TOK_BLOCK == 0 and D == D_MODEL
    N = B * T

    mod_rows = (B + 1 + 7) // 8 * 8
    cc = jnp.zeros((mod_rows, D), F32).at[:B].set(c).at[B].set(c_ctx)
    mod = _modulation(cc, w_mod[0], b_mod[0][None])
    mod3 = mod.reshape(mod_rows, 1, N_MOD * D)
    zpad = jnp.zeros((2, 64, RWKV_WIDTH), F32)
    wup_pad = jnp.concatenate([rwkv_w_up[0], zpad], axis=1)
    aup_pad = jnp.concatenate([zpad, rwkv_a_up[0]], axis=1)
    row = lambda a: a.reshape(1, -1)
    qk_w = jnp.concatenate([jnp.tile(q_norm_w[0], DIFF_WIDTH // HEAD_DIM),
                            jnp.tile(k_norm_w[0], DIFF_WIDTH // HEAD_DIM)])[None]
    lam_p = jnp.stack([lam_q1[0], lam_k1[0], lam_q2[0], lam_k2[0]])
    rw = jnp.zeros((D, ROUTER_LANES), F32).at[:, :N_GROUPS].set(w_group[0])
    rw = rw.at[:, N_GROUPS:N_GROUPS + N_EXPERTS].set(w_expert[0])
    rw_hi = rw.astype(BF16)
    rw_lo = (rw - rw_hi.astype(F32)).astype(BF16)
    rb = jnp.zeros((1, ROUTER_LANES), F32).at[0, :N_GROUPS].set(b_group[0])
    rb = rb.at[0, N_GROUPS:N_GROUPS + N_EXPERTS].set(b_expert[0])
    cos, sin = _rope_tables(T)

    pr, pd = _inproj(ctx, x, mod3, row(norm1_w[0]), w_in[0].astype(BF16))

    r, v, kn, lw, kd, bb, g, bonus = _rwkv_prep(
        pr, shift_w[0], rwkv_w0[0][:, None, :], wup_pad, rwkv_a0[0][:, None, :], aup_pad,
        rwkv_g_up[0], row(rwkv_k_k[0]), row(rwkv_k_a[0]), row(rwkv_r_k[0]), _block_ones(RWKV_WIDTH))
    y_f = _rwkv_scan(0, r, v, kn, lw, kd, bb)
    y_b = _rwkv_scan(1, r, v, kn, lw, kd, bb)
    o_r = _rwkv_readout(y_f, y_b, g, bonus, row(rwkv_ln_w[0]), row(rwkv_ln_b[0]), _block_ones(LANES))

    qk = _qk_prep(pd, qk_w, cos, sin, _block_ones(2 * DIFF_WIDTH))
    o_d = _diff_attn(qk, pd, lam_p, row(subln_w[0]))

    wo = w_out[0].astype(BF16)
    x1, h2, route = _outproj(o_r, o_d, x, wo[:RWKV_WIDTH], wo[RWKV_WIDTH:], mod3,
                             row(norm2_w[0]), rw_hi, rw_lo, rb)

    R = ROUTE_ROWS
    flat_e = route.reshape(N, ROUTER_LANES)[:, :2].astype(jnp.int32).reshape(2 * N)
    onehot = (flat_e[:, None] == jnp.arange(N_EXPERTS, dtype=jnp.int32)[None, :]).astype(jnp.int32)
    csum = jnp.cumsum(onehot, axis=0)
    rank = jnp.take_along_axis(csum, flat_e[:, None], axis=1)[:, 0] - 1
    counts = csum[-1]
    padded = (counts + R - 1) // R * R
    pad_end = jnp.cumsum(padded)
    pad_start = pad_end - padded
    dest = (pad_start[flat_e] + rank).astype(jnp.int32)
    n_blocks = (2 * N + N_EXPERTS * (R - 1) + R - 1) // R
    block_expert = jnp.minimum(
        jnp.searchsorted(pad_end, jnp.arange(n_blocks, dtype=jnp.int32) * R, side='right'),
        N_EXPERTS - 1).astype(jnp.int32)
    dest3 = dest.reshape(N // TOK_BLOCK, 1, 2 * TOK_BLOCK)

    xs = _moe_scatter(dest3, h2.reshape(N, D), jnp.zeros((n_blocks * R, D), F32))
    ys = _moe_experts(block_expert, xs, moe_w_gate[0], moe_w_up[0], moe_w_down[0])
    out = _moe_combine(dest3, x1.reshape(N, D), route.reshape(N, ROUTER_LANES), mod3, ys,
                       T // TOK_BLOCK)
    return out.reshape(B, T, D)
```

```python
import functools
import math

import jax
import jax.numpy as jnp
from jax import lax
from jax.experimental import pallas as pl
from jax.experimental.pallas import tpu as pltpu

F32 = jnp.float32
BF16 = jnp.bfloat16

D_MODEL = 1024
HEAD_DIM = 64
RWKV_WIDTH = 512
RWKV_HEADS = 8
DIFF_WIDTH = 512
DIFF_HEADS = 4
DIR_LORA = 256
RWKV_IN = 2048
DIFF_IN = 1536
IN_WIDTH = RWKV_IN + DIFF_IN
GRID_W = 64
AXIS_DIM = HEAD_DIM // 2
ROPE_THETA = 10000.0
DIFF_SCALE = HEAD_DIM ** -0.5
N_GROUPS = 4
EXPERTS_PER_GROUP = 8
N_EXPERTS = 32
EXPERT_FF = 512
NORM_EPS = 1e-6
GN_EPS = 64e-5
N_MOD = 6
LAM_INIT = 0.8 - 0.6 * math.exp(-0.3 * 0)

TOK_BLOCK = 256
CHUNK = 64
CHUNKS_PER_BLOCK = TOK_BLOCK // CHUNK
HEAD_PAIRS = RWKV_HEADS // 2
SCAN_PAIRS = 4
LANES = 128
SUBL = 8
ROUTE_ROWS = 256
ROUTER_LANES = 128
ATTN_TQ = 512
ATTN_SUB = 64
LOG2E = math.log2(math.e)
VMEM_LIMIT = 56 * 1024 * 1024


def _cparams(sem):
    return pltpu.CompilerParams(dimension_semantics=sem, vmem_limit_bytes=VMEM_LIMIT)


def _mod_kernel(c_ref, w_ref, b_ref, o_ref):
    c = c_ref[...]
    s = c * jax.nn.sigmoid(c)
    o_ref[...] = jnp.dot(s, w_ref[...], preferred_element_type=F32,
                         precision=lax.Precision.HIGHEST) + b_ref[...]


def _modulation(cc, w_mod, b_mod):
    rows = cc.shape[0]
    ncol = w_mod.shape[1] // D_MODEL
    return pl.pallas_call(
        _mod_kernel,
        out_shape=jax.ShapeDtypeStruct((rows, w_mod.shape[1]), F32),
        grid=(ncol,),
        in_specs=[pl.BlockSpec((rows, D_MODEL), lambda j: (0, 0)),
                  pl.BlockSpec((D_MODEL, D_MODEL), lambda j: (0, j)),
                  pl.BlockSpec((1, D_MODEL), lambda j: (0, j))],
        out_specs=pl.BlockSpec((rows, D_MODEL), lambda j: (0, j)),
        compiler_params=_cparams(("arbitrary",)),
        name="modulation",
    )(cc, w_mod, b_mod)


def _inproj_kernel(ctx_ref, x_ref, sh_ref, sc_ref, nw_ref, w_ref, pr_ref, pd_ref):
    i = pl.program_id(1)
    xin = jnp.where(i == 0, ctx_ref[0], x_ref[0])
    ms = jnp.mean(xin * xin, axis=-1, keepdims=True)
    y = xin * lax.rsqrt(ms + NORM_EPS) * nw_ref[...]
    h = (y * (1.0 + sc_ref[0]) + sh_ref[0]).astype(BF16)
    p = jnp.dot(h, w_ref[...], preferred_element_type=F32)
    pr_ref[0] = p[:, :RWKV_IN].astype(BF16)
    pd_ref[0] = p[:, RWKV_IN:].astype(BF16)


def _mod_row(nb):
    return lambda b, i: jnp.where(i == 0, nb, b)


def _inproj(ctx, x, mod3, norm1_w, w_in_bf):
    B, T, D = x.shape
    nblk = 1 + T // TOK_BLOCK
    S = nblk * TOK_BLOCK
    row = _mod_row(B)
    return pl.pallas_call(
        _inproj_kernel,
        out_shape=(jax.ShapeDtypeStruct((B, S, RWKV_IN), BF16),
                   jax.ShapeDtypeStruct((B, S, DIFF_IN), BF16)),
        grid=(B, nblk),
        in_specs=[pl.BlockSpec((1, TOK_BLOCK, D), lambda b, i: (b, 0, 0)),
                  pl.BlockSpec((1, TOK_BLOCK, D), lambda b, i: (b, jnp.maximum(i - 1, 0), 0)),
                  pl.BlockSpec((1, 1, D), lambda b, i: (row(b, i), 0, 0)),
                  pl.BlockSpec((1, 1, D), lambda b, i: (row(b, i), 0, 1)),
                  pl.BlockSpec((1, D), lambda b, i: (0, 0)),
                  pl.BlockSpec((D, IN_WIDTH), lambda b, i: (0, 0))],
        out_specs=(pl.BlockSpec((1, TOK_BLOCK, RWKV_IN), lambda b, i: (b, i, 0)),
                   pl.BlockSpec((1, TOK_BLOCK, DIFF_IN), lambda b, i: (b, i, 0))),
        compiler_params=_cparams(("parallel", "arbitrary")),
        name="inproj",
    )(ctx, x, mod3, mod3, norm1_w, w_in_bf)


def _head_sum(v, e_ref):
    return jnp.dot(v, e_ref[...], preferred_element_type=F32)


def _rwkv_prep_kernel(nblk, z_ref, zp_ref, zn_ref, sw_ref, w0_ref, wup_ref, a0_ref, aup_ref,
                      gup_ref, kk_ref, ka_ref, rk_ref, e_ref,
                      r_ref, v_ref, kn_ref, lw_ref, kd_ref, bb_ref, g_ref, bo_ref):
    i = pl.program_id(1)
    z = z_ref[0].astype(F32)
    row = lax.broadcasted_iota(jnp.int32, (TOK_BLOCK, 1), 0)
    prev_ok = jnp.logical_and(i != 0, i != 1)
    next_ok = jnp.logical_and(i != 0, i != nblk - 1)
    zp = zp_ref[0][15:16, :].astype(F32) * prev_ok.astype(F32)
    zn = zn_ref[0][0:1, :].astype(F32) * next_ok.astype(F32)
    prev = jnp.where(row == 0, zp, pltpu.roll(z, 1, 0))
    nxt = jnp.where(row == TOK_BLOCK - 1, zn, pltpu.roll(z, TOK_BLOCK - 1, 0))
    sw = sw_ref[...]
    rx = sw[0:1] * prev + sw[1:2] * z + sw[2:3] * nxt
    r = rx[:, 0:512]
    k = rx[:, 512:1024]
    v = rx[:, 1024:1536]
    kkf = k * kk_ref[...]
    nrm = jnp.sqrt(_head_sum(kkf * kkf, e_ref))
    kn = kkf / jnp.maximum(nrm, 1e-12)
    for p in range(HEAD_PAIRS):
        sl = slice(p * LANES, (p + 1) * LANES)
        r_ref[0, p] = r[:, sl].astype(BF16)
        v_ref[0, p] = v[:, sl].astype(BF16)
        kn_ref[0, p] = kn[:, sl].astype(BF16)
    for d in range(2):
        base = 1536 + d * DIR_LORA
        x128 = rx[:, base:base + 128]
        lg = rx[:, base + 128:base + 256]
        w_raw = w0_ref[d] + jnp.dot(jnp.tanh(x128), wup_ref[d], preferred_element_type=F32)
        u = -w_raw
        sp = jnp.maximum(u, 0.0) + jnp.log(1.0 + jnp.exp(-jnp.abs(u)))
        logw = -jnp.exp(-sp - 0.5)
        a = jax.nn.sigmoid(a0_ref[d] + jnp.dot(x128, aup_ref[d], preferred_element_type=F32))
        g = jnp.dot(jax.nn.sigmoid(lg), gup_ref[d], preferred_element_type=F32)
        kd = k * (1.0 + (a - 1.0) * ka_ref[...])
        bonus = _head_sum(r * kd * rk_ref[...], e_ref) * v
        bb = kn * a
        for p in range(HEAD_PAIRS):
            sl = slice(p * LANES, (p + 1) * LANES)
            lw_ref[d, 0, p] = logw[:, sl]
            kd_ref[d, 0, p] = kd[:, sl].astype(BF16)
            bb_ref[d, 0, p] = bb[:, sl].astype(BF16)
        g_ref[d, 0] = g.astype(BF16)
        bo_ref[d, 0] = bonus.astype(BF16)


def _rwkv_prep(pr, shift_w, w0, wup_pad, a0, aup_pad, g_up, k_k, k_a, r_k, e512):
    B, S, _ = pr.shape
    nblk = S // TOK_BLOCK
    hb = TOK_BLOCK // 16
    nh = S // 16
    pair = lambda dt: jax.ShapeDtypeStruct((B, HEAD_PAIRS, S, LANES), dt)
    pair2 = lambda dt: jax.ShapeDtypeStruct((2, B, HEAD_PAIRS, S, LANES), dt)
    chan2 = jax.ShapeDtypeStruct((2, B, S, RWKV_WIDTH), BF16)
    pspec = pl.BlockSpec((1, HEAD_PAIRS, TOK_BLOCK, LANES), lambda b, i: (b, 0, i, 0))
    pspec2 = pl.BlockSpec((2, 1, HEAD_PAIRS, TOK_BLOCK, LANES), lambda b, i: (0, b, 0, i, 0))
    cspec2 = pl.BlockSpec((2, 1, TOK_BLOCK, RWKV_WIDTH), lambda b, i: (0, b, i, 0))
    full = lambda a: pl.BlockSpec(a.shape, lambda b, i: (0,) * a.ndim)
    return pl.pallas_call(
        functools.partial(_rwkv_prep_kernel, nblk),
        out_shape=(pair(BF16), pair(BF16), pair(BF16), pair2(F32), pair2(BF16), pair2(BF16),
                   chan2, chan2),
        grid=(B, nblk),
        in_specs=[pl.BlockSpec((1, TOK_BLOCK, RWKV_IN), lambda b, i: (b, i, 0)),
                  pl.BlockSpec((1, 16, RWKV_IN), lambda b, i: (b, jnp.maximum(i * hb - 1, 0), 0)),
                  pl.BlockSpec((1, 16, RWKV_IN), lambda b, i: (b, jnp.minimum((i + 1) * hb, nh - 1), 0)),
                  full(shift_w), full(w0), full(wup_pad), full(a0), full(aup_pad), full(g_up),
                  full(k_k), full(k_a), full(r_k), full(e512)],
        out_specs=(pspec, pspec, pspec, pspec2, pspec2, pspec2, cspec2, cspec2),
        compiler_params=_cparams(("parallel", "arbitrary")),
        name="rwkv_prep",
    )(pr, pr, pr, shift_w, w0, wup_pad, a0, aup_pad, g_up, k_k, k_a, r_k, e512)


def _bmm(a, b):
    return jnp.einsum('nij,njk->nik', a.astype(BF16), b.astype(BF16), preferred_element_type=F32)


def _bmm_nt(a, b):
    return jnp.einsum('nij,nkj->nik', a.astype(BF16), b.astype(BF16), preferred_element_type=F32)


def _mm(a, b):
    return jnp.dot(a.astype(BF16), b.astype(BF16), preferred_element_type=F32)


def _rwkv_scan_kernel(reverse, r_ref, v_ref, kn_ref, lw_ref, kd_ref, bb_ref, y_ref,
                      h_ref, q_s, y0_s, m_s, n_s, pt_s):
    C = CHUNK
    NC = CHUNKS_PER_BLOCK

    @pl.when(pl.program_id(1) == 0)
    def _():
        h_ref[...] = jnp.zeros_like(h_ref)

    t_i = lax.broadcasted_iota(jnp.int32, (C, C), 0)
    s_i = lax.broadcasted_iota(jnp.int32, (C, C), 1)
    if reverse:
        strict = s_i > t_i
        incl = s_i >= t_i
    else:
        strict = s_i < t_i
        incl = s_i <= t_i
    blk16 = (t_i // 16) == (s_i // 16)
    blk32 = (t_i // 32) == (s_i // 32)
    eye = (t_i == s_i).astype(F32)
    G = SCAN_PAIRS
    NB = G * NC
    tri = jnp.broadcast_to(incl.astype(F32)[None], (NB, C, C))
    lane = lax.broadcasted_iota(jnp.int32, (1, 1, LANES), 2)
    head0 = lane < HEAD_DIM
    rr = lax.broadcasted_iota(jnp.int32, (LANES, LANES), 0)
    cc = lax.broadcasted_iota(jnp.int32, (LANES, LANES), 1)
    same_head = ((rr // HEAD_DIM) == (cc // HEAD_DIM))[None]

    def chunked(blk):
        return blk.astype(F32).reshape(NB, C, LANES)

    def group_body(gi, carry):
        ps = pl.ds(gi * G, G)
        r = chunked(r_ref[0, ps])
        v = chunked(v_ref[0, ps])
        kn = chunked(kn_ref[0, ps])
        kd = chunked(kd_ref[0, 0, ps])
        bb = chunked(bb_ref[0, 0, ps])
        logw = lw_ref[0, 0, ps].reshape(NB, C, LANES)
        lw_hi = logw.astype(BF16).astype(F32)
        lw_lo = logw - lw_hi
        L = _bmm(tri, lw_hi) + _bmm(tri, lw_lo)
        Lex = L - logw
        tot = L[:, 0:1, :] if reverse else L[:, C - 1:C, :]
        e_in = jnp.exp(-L)
        e_hat = jnp.exp(tot - L)
        at = -kn * jnp.exp(Lex)
        rt = r * jnp.exp(L)
        bt = bb * e_in
        kt = kd * e_in
        bh = bb * e_hat
        kh = kd * e_hat
        lhs = jnp.concatenate([at, rt], axis=1)
        rhs = jnp.concatenate([bt, kt], axis=1)
        sc = jnp.concatenate([_bmm_nt(jnp.where(head0, lhs, 0.0), rhs),
                              _bmm_nt(jnp.where(head0, 0.0, lhs), rhs)], axis=0)
        a_ab = jnp.where(strict, sc[:, :C, :C], 0.0)
        a_ak = jnp.where(strict, sc[:, :C, C:], 0.0)
        a_rb = jnp.where(incl, sc[:, C:, :C], 0.0)
        a_rk = jnp.where(incl, sc[:, C:, C:], 0.0)
        nd = jnp.where(blk16, a_ab, 0.0)
        T = eye + nd
        pw = _bmm(nd, nd)
        T = T + _bmm(pw, T)
        pw = _bmm(pw, pw)
        T = T + _bmm(pw, T)
        pw = _bmm(pw, pw)
        T = T + _bmm(pw, T)
        off = jnp.where(jnp.logical_and(blk32, jnp.logical_not(blk16)), a_ab, 0.0)
        T = T + _bmm(T, _bmm(off, T))
        off = jnp.where(blk32, 0.0, a_ab)
        T = T + _bmm(T, _bmm(off, T))
        v2 = jnp.concatenate([v, v], axis=0)
        at2 = jnp.concatenate([at, at], axis=0)
        akv = _bmm(a_ak, v2)
        arkv = _bmm(a_rk, v2)
        wu = _bmm(T, jnp.concatenate([at2, akv], axis=2))
        qy = _bmm(a_rb, wu)
        merge = lambda zz: jnp.where(head0, zz[:NB], zz[NB:])
        wup = jnp.concatenate([merge(wu[:, :, :LANES]), merge(wu[:, :, LANES:])], axis=2)
        q_s[...] = (rt + merge(qy[:, :, :LANES])).reshape(G, NC, C, LANES)
        y0_s[...] = merge(qy[:, :, LANES:] + arkv).reshape(G, NC, C, LANES)
        bht = jnp.swapaxes(bh, 1, 2)
        kht = jnp.swapaxes(kh, 1, 2)
        mn = _bmm(bht, wup)
        m_s[...] = jnp.where(same_head, mn[:, :, :LANES], 0.0).reshape(G, NC, LANES, LANES)
        n_s[...] = jnp.where(same_head, mn[:, :, LANES:] + _bmm(kht, v), 0.0).reshape(G, NC, LANES, LANES)
        pt_s[...] = jnp.exp(jnp.sum(jnp.swapaxes(logw, 1, 2), axis=2, keepdims=True)).reshape(G, NC, LANES, 1)
        for j in range(NC):
            c = NC - 1 - j if reverse else j
            H = h_ref[ps]
            y_ref[0, ps, c * C:(c + 1) * C, :] = _bmm(q_s[:, c], H) + y0_s[:, c]
            h_ref[ps] = pt_s[:, c] * H + _bmm(m_s[:, c], H) + n_s[:, c]
        return carry

    if G == HEAD_PAIRS:
        group_body(0, 0)
    else:
        lax.fori_loop(0, HEAD_PAIRS // G, group_body, 0)


def _rwkv_scan(d, r, v, kn, lw, kd, bb):
    B, _, S, _ = r.shape
    nblk = S // TOK_BLOCK
    reverse = d == 1
    if reverse:
        blk = lambda i: jnp.where(i == 0, 0, nblk - i)
    else:
        blk = lambda i: i
    pspec = pl.BlockSpec((1, HEAD_PAIRS, TOK_BLOCK, LANES), lambda b, i: (b, 0, blk(i), 0))
    pspec_d = pl.BlockSpec((1, 1, HEAD_PAIRS, TOK_BLOCK, LANES), lambda b, i: (d, b, 0, blk(i), 0))
    NC, C, G = CHUNKS_PER_BLOCK, CHUNK, SCAN_PAIRS
    return pl.pallas_call(
        functools.partial(_rwkv_scan_kernel, reverse),
        out_shape=jax.ShapeDtypeStruct((B, HEAD_PAIRS, S, LANES), F32),
        grid=(B, nblk),
        in_specs=[pspec, pspec, pspec, pspec_d, pspec_d, pspec_d],
        out_specs=pspec,
        scratch_shapes=[pltpu.VMEM((HEAD_PAIRS, LANES, LANES), F32),
                        pltpu.VMEM((G, NC, C, LANES), F32),
                        pltpu.VMEM((G, NC, C, LANES), F32),
                        pltpu.VMEM((G, NC, LANES, LANES), F32),
                        pltpu.VMEM((G, NC, LANES, LANES), F32),
                        pltpu.VMEM((G, NC, LANES, 1), F32)],
        compiler_params=_cparams(("parallel", "arbitrary")),
        name="rwkv_scan_rev" if reverse else "rwkv_scan_fwd",
    )(r, v, kn, lw, kd, bb)


def _rwkv_readout_kernel(y0_ref, y1_ref, g_ref, bo_ref, lnw_ref, lnb_ref, e_ref, o_ref):
    inv = 1.0 / HEAD_DIM
    for p in range(HEAD_PAIRS):
        sl = slice(p * LANES, (p + 1) * LANES)
        acc = None
        for d, y_ref in enumerate((y0_ref, y1_ref)):
            y = y_ref[0, p]
            mu = _head_sum(y, e_ref) * inv
            dl = y - mu
            var = _head_sum(dl * dl, e_ref) * inv
            yn = dl * lax.rsqrt(var + GN_EPS) * lnw_ref[:, sl] + lnb_ref[:, sl]
            o = (yn + bo_ref[d, 0, :, sl].astype(F32)) * g_ref[d, 0, :, sl].astype(F32)
            acc = o if acc is None else acc + o
        o_ref[0, :, sl] = acc.astype(BF16)


def _rwkv_readout(y0, y1, g, bonus, ln_w, ln_b, e128):
    B, _, S, _ = y0.shape
    T = S - TOK_BLOCK
    yspec = pl.BlockSpec((1, HEAD_PAIRS, TOK_BLOCK, LANES), lambda b, i: (b, 0, i + 1, 0))
    cspec2 = pl.BlockSpec((2, 1, TOK_BLOCK, RWKV_WIDTH), lambda b, i: (0, b, i + 1, 0))
    full = lambda a: pl.BlockSpec(a.shape, lambda b, i: (0,) * a.ndim)
    return pl.pallas_call(
        _rwkv_readout_kernel,
        out_shape=jax.ShapeDtypeStruct((B, T, RWKV_WIDTH), BF16),
        grid=(B, T // TOK_BLOCK),
        in_specs=[yspec, yspec, cspec2, cspec2, full(ln_w), full(ln_b), full(e128)],
        out_specs=pl.BlockSpec((1, TOK_BLOCK, RWKV_WIDTH), lambda b, i: (b, i, 0)),
        compiler_params=_cparams(("parallel", "arbitrary")),
        name="rwkv_readout",
    )(y0, y1, g, bonus, ln_w, ln_b, e128)


def _qk_prep_kernel(z_ref, nw_ref, cos_ref, sin_ref, e_ref, q_ref, k_ref):
    z = z_ref[0].astype(F32)
    ms = _head_sum(z * z, e_ref) * (1.0 / HEAD_DIM)
    zn = z * lax.rsqrt(ms + NORM_EPS) * nw_ref[...]
    cos = cos_ref[...]
    sin = sin_ref[...]
    lane = lax.broadcasted_iota(jnp.int32, (1, LANES), 1)
    first = (lane % (2 * 16)) < 16
    for j in range(2 * DIFF_WIDTH // LANES):
        sl = slice(j * LANES, (j + 1) * LANES)
        t = zn[:, sl]
        rot = jnp.where(first, -pltpu.roll(t, LANES - 16, 1), pltpu.roll(t, 16, 1))
        out = t * cos + rot * sin
        nq = DIFF_WIDTH // LANES
        if j < nq:
            q_ref[0, :, sl] = (out * (DIFF_SCALE * LOG2E)).astype(BF16)
        else:
            k_ref[0, :, (j - nq) * LANES:(j - nq + 1) * LANES] = out.astype(BF16)


def _qk_prep(pd, qk_w, cos, sin, e1024):
    B, S, _ = pd.shape
    full = lambda a: pl.BlockSpec(a.shape, lambda b, i: (0,) * a.ndim)
    return pl.pallas_call(
        _qk_prep_kernel,
        out_shape=(jax.ShapeDtypeStruct((B, S - TOK_BLOCK, DIFF_WIDTH), BF16),
                   jax.ShapeDtypeStruct((B, S, DIFF_WIDTH), BF16)),
        grid=(B, S // TOK_BLOCK),
        in_specs=[pl.BlockSpec((1, TOK_BLOCK, 2 * DIFF_WIDTH), lambda b, i: (b, i, 0)),
                  full(qk_w),
                  pl.BlockSpec((TOK_BLOCK, LANES), lambda b, i: (i, 0)),
                  pl.BlockSpec((TOK_BLOCK, LANES), lambda b, i: (i, 0)),
                  full(e1024)],
        out_specs=(pl.BlockSpec((1, TOK_BLOCK, DIFF_WIDTH), lambda b, i: (b, jnp.maximum(i - 1, 0), 0)),
                   pl.BlockSpec((1, TOK_BLOCK, DIFF_WIDTH), lambda b, i: (b, i, 0))),
        compiler_params=_cparams(("parallel", "arbitrary")),
        name="qk_prep",
    )(pd, qk_w, cos, sin, e1024)


def _diff_attn_kernel(nkv, q_ref, k_ref, v_ref, lam_ref, sw_ref, o_ref,
                      qs_ref, s0_ref, s1_ref, p0_ref, p1_ref, al0_ref, al1_ref, m_ref, l_ref, acc_ref):
    TQ = ATTN_TQ
    TK = TOK_BLOCK
    q = q_ref[0]
    lane = lax.broadcasted_iota(jnp.int32, (1, LANES), 1)
    zero = jnp.zeros_like(q)
    qs_ref[0:TQ, :] = jnp.where(lane < HEAD_DIM, q, zero)
    qs_ref[TQ:2 * TQ, :] = jnp.where(lane < HEAD_DIM, zero, q)
    m_ref[...] = jnp.full(m_ref.shape, -1e30, F32)
    l_ref[...] = jnp.zeros(l_ref.shape, F32)
    acc_ref[...] = jnp.zeros(acc_ref.shape, F32)
    p1_ref[...] = jnp.zeros(p1_ref.shape, BF16)
    al1_ref[...] = jnp.ones(al1_ref.shape, F32)

    def chunk(ref, j):
        return ref[0, pl.ds(pl.multiple_of(j * TK, TK), TK), :]

    def scores(j, s_ref):
        s_ref[...] = lax.dot_general(qs_ref[...], chunk(k_ref, j), (((1,), (1,)), ((), ())),
                                     preferred_element_type=F32)

    def softmax_step(s_ref, p_ref, al_ref):
        for r in range(2 * TQ // ATTN_SUB):
            rows = slice(r * ATTN_SUB, (r + 1) * ATTN_SUB)
            s = s_ref[rows, :]
            m_old = m_ref[rows, :]
            mx = jnp.maximum(s[:, :LANES], s[:, LANES:])
            m_new = jnp.maximum(m_old, jnp.max(mx, axis=1, keepdims=True))
            al = jnp.exp2(m_old - m_new)
            p = jnp.exp2(s - jnp.concatenate([m_new, m_new], axis=1))
            l_ref[rows, :] = al * l_ref[rows, :] + (p[:, :LANES] + p[:, LANES:])
            m_ref[rows, :] = m_new
            al_ref[rows, :] = al
            p_ref[rows, :] = p.astype(BF16)

    def values(j, p_ref, al_ref):
        acc_ref[...] = al_ref[...] * acc_ref[...] + jnp.dot(p_ref[...], chunk(v_ref, j),
                                                             preferred_element_type=F32)

    scores(0, s0_ref)

    def pair(t, carry):
        j = 2 * t
        scores(j + 1, s1_ref)
        softmax_step(s0_ref, p0_ref, al0_ref)
        values(jnp.maximum(j - 1, 0), p1_ref, al1_ref)
        scores(jnp.minimum(j + 2, nkv - 1), s0_ref)
        softmax_step(s1_ref, p1_ref, al1_ref)
        values(j, p0_ref, al0_ref)
        return carry

    lax.fori_loop(0, nkv // 2, pair, 0)
    if nkv % 2:
        softmax_step(s0_ref, p0_ref, al0_ref)
        if nkv > 1:
            values(nkv - 2, p1_ref, al1_ref)
        values(nkv - 1, p0_ref, al0_ref)
    else:
        values(nkv - 1, p1_ref, al1_ref)

    o = acc_ref[...] / jnp.sum(l_ref[...], axis=1, keepdims=True)
    lp = lam_ref[...]
    lam = (jnp.exp(jnp.sum(lp[0:1] * lp[1:2], axis=1, keepdims=True))
           - jnp.exp(jnp.sum(lp[2:3] * lp[3:4], axis=1, keepdims=True)) + LAM_INIT)
    od = o[:TQ] - lam * o[TQ:]
    ms = jnp.mean(od * od, axis=-1, keepdims=True)
    o_ref[0] = (od * lax.rsqrt(ms + NORM_EPS) * sw_ref[...] * (1.0 - LAM_INIT)).astype(BF16)


def _diff_attn(q, k, pd, lam_p, subln_w):
    B, S, _ = k.shape
    T = S - TOK_BLOCK
    H = DIFF_HEADS
    TQ = ATTN_TQ
    TQ2 = 2 * TQ
    return pl.pallas_call(
        functools.partial(_diff_attn_kernel, S // TOK_BLOCK),
        out_shape=jax.ShapeDtypeStruct((B, T, DIFF_WIDTH), BF16),
        grid=(B, H, T // TQ),
        in_specs=[pl.BlockSpec((1, TQ, LANES), lambda b, h, i: (b, i, h)),
                  pl.BlockSpec((1, S, LANES), lambda b, h, i: (b, 0, h)),
                  pl.BlockSpec((1, S, LANES), lambda b, h, i: (b, 0, 2 * H + h)),
                  pl.BlockSpec(lam_p.shape, lambda b, h, i: (0, 0)),
                  pl.BlockSpec(subln_w.shape, lambda b, h, i: (0, 0))],
        out_specs=pl.BlockSpec((1, TQ, LANES), lambda b, h, i: (b, i, h)),
        scratch_shapes=[pltpu.VMEM((TQ2, LANES), BF16),
                        pltpu.VMEM((TQ2, 2 * LANES), F32), pltpu.VMEM((TQ2, 2 * LANES), F32),
                        pltpu.VMEM((TQ2, 2 * LANES), BF16), pltpu.VMEM((TQ2, 2 * LANES), BF16),
                        pltpu.VMEM((TQ2, LANES), F32), pltpu.VMEM((TQ2, LANES), F32),
                        pltpu.VMEM((TQ2, LANES), F32), pltpu.VMEM((TQ2, LANES), F32),
                        pltpu.VMEM((TQ2, LANES), F32)],
        compiler_params=_cparams(("parallel", "parallel", "arbitrary")),
        name="diff_attn",
    )(q, k, pd, lam_p, subln_w)


def _outproj_kernel(or_ref, od_ref, x_ref, wr_ref, wd_ref, g1_ref, sh_ref, sc_ref, nw_ref,
                    rw_ref, rb_ref, x1_ref, h2_ref, rt_ref, cnt_ref, run_ref):
    attn = (jnp.dot(or_ref[0], wr_ref[...], preferred_element_type=F32)
            + jnp.dot(od_ref[0], wd_ref[...], preferred_element_type=F32))
    x1 = x_ref[0] + g1_ref[0] * attn
    x1_ref[0] = x1
    ms = jnp.mean(x1 * x1, axis=-1, keepdims=True)
    h2 = x1 * lax.rsqrt(ms + NORM_EPS) * nw_ref[...] * (1.0 + sc_ref[0]) + sh_ref[0]
    _to_tiles(h2_ref, h2, TOK_BLOCK)
    hh = h2.astype(BF16)
    hl = (h2 - hh.astype(F32)).astype(BF16)
    hw = jnp.dot(hh, rw_ref[...], preferred_element_type=F32)
    logits = (hw[:, :ROUTER_LANES] + hw[:, ROUTER_LANES:]
              + jnp.dot(hl, rw_ref[:, :ROUTER_LANES], preferred_element_type=F32)) + rb_ref[...]
    lane = lax.broadcasted_iota(jnp.int32, logits.shape, 1).astype(F32)
    neg = -jnp.inf
    big = 1e9
    gl = jnp.where(lane < N_GROUPS, logits, neg)
    gmax = jnp.max(gl, axis=1, keepdims=True)
    gsel = jnp.min(jnp.where(gl == gmax, lane, big), axis=1, keepdims=True)
    gprob = 1.0 / jnp.sum(jnp.exp(gl - gmax), axis=1, keepdims=True)
    lo = N_GROUPS + EXPERTS_PER_GROUP * gsel
    el = jnp.where(jnp.logical_and(lane >= lo, lane < lo + EXPERTS_PER_GROUP), logits, neg)
    v1 = jnp.max(el, axis=1, keepdims=True)
    i1 = jnp.min(jnp.where(el == v1, lane, big), axis=1, keepdims=True)
    el2 = jnp.where(lane == i1, neg, el)
    v2 = jnp.max(el2, axis=1, keepdims=True)
    i2 = jnp.min(jnp.where(el2 == v2, lane, big), axis=1, keepdims=True)
    e21 = jnp.exp(v2 - v1)
    w1 = gprob / (1.0 + e21)
    w2 = gprob * e21 / (1.0 + e21)
    @pl.when(jnp.logical_and(pl.program_id(0) == 0, pl.program_id(1) == 0))
    def _():
        run_ref[...] = jnp.zeros_like(run_ref)

    oh1 = (lane == i1).astype(BF16)
    oh2 = (lane == i2).astype(BF16)
    tr = lax.broadcasted_iota(jnp.int32, (TOK_BLOCK, TOK_BLOCK), 0)
    tc = lax.broadcasted_iota(jnp.int32, (TOK_BLOCK, TOK_BLOCK), 1)
    before = (tc < tr).astype(BF16)
    run = run_ref[...]
    tot1 = jnp.sum(oh1.astype(F32), axis=0, keepdims=True)
    tot2 = jnp.sum(oh2.astype(F32), axis=0, keepdims=True)
    c1 = jnp.dot(before, oh1, preferred_element_type=F32) + run
    c2 = jnp.dot(before, oh2, preferred_element_type=F32) + (run + tot1)
    rank1 = jnp.sum(jnp.where(lane == i1, c1, 0.0), axis=1, keepdims=True)
    rank2 = jnp.sum(jnp.where(lane == i2, c2, 0.0), axis=1, keepdims=True)
    run = run + tot1 + tot2
    run_ref[...] = run
    cnt_ref[...] = jnp.broadcast_to(run, cnt_ref.shape)
    vals = (i1 - N_GROUPS, i2 - N_GROUPS, w1, w2, rank1, rank2)
    out = jnp.zeros_like(logits)
    for n, val in enumerate(vals):
        out = jnp.where(lane == n, val, out)
    rt_ref[0] = out


def _outproj(o_r, o_d, x, w_out_r, w_out_d, mod3, norm2_w, rw_hl, rb):
    B, T, D = x.shape
    full = lambda a: pl.BlockSpec(a.shape, lambda b, i: (0,) * a.ndim)
    modspec = lambda col: pl.BlockSpec((1, 1, D), lambda b, i: (b, 0, col))
    tok = lambda w: pl.BlockSpec((1, TOK_BLOCK, w), lambda b, i: (b, i, 0))
    return pl.pallas_call(
        _outproj_kernel,
        out_shape=(jax.ShapeDtypeStruct((B, T, D), F32),
                   jax.ShapeDtypeStruct((B * T * SUBL, LANES), F32),
                   jax.ShapeDtypeStruct((B, T, ROUTER_LANES), F32),
                   jax.ShapeDtypeStruct((8, ROUTER_LANES), F32)),
        grid=(B, T // TOK_BLOCK),
        in_specs=[tok(RWKV_WIDTH), tok(DIFF_WIDTH), tok(D), full(w_out_r), full(w_out_d),
                  modspec(2), modspec(3), modspec(4), full(norm2_w),
                  full(rw_hl), full(rb)],
        out_specs=(tok(D), pl.BlockSpec((TOK_BLOCK * SUBL, LANES), lambda b, i: (b * (T // TOK_BLOCK) + i, 0)),
                   tok(ROUTER_LANES), pl.BlockSpec((8, ROUTER_LANES), lambda b, i: (0, 0))),
        scratch_shapes=[pltpu.VMEM((1, ROUTER_LANES), F32)],
        compiler_params=_cparams(("arbitrary", "arbitrary")),
        name="outproj_router",
    )(o_r, o_d, x, w_out_r, w_out_d, mod3, mod3, mod3, norm2_w, rw_hl, rb)


def _tile_copy(src_ref, src_tok, dst_ref, dst_tok, sem):
    return pltpu.make_async_copy(src_ref.at[pl.ds(pl.multiple_of(src_tok * SUBL, SUBL), SUBL), :],
                                 dst_ref.at[pl.ds(pl.multiple_of(dst_tok * SUBL, SUBL), SUBL), :], sem)


def _from_tiles(ref, n):
    return jnp.concatenate([ref[pl.ds(j, n, stride=SUBL), :] for j in range(SUBL)], axis=1)


def _to_tiles(ref, val, n):
    for j in range(SUBL):
        ref[pl.ds(j, n, stride=SUBL), :] = val[:, j * LANES:(j + 1) * LANES]


def _moe_scatter_kernel(dest_ref, h_ref, xs_in_ref, xs_ref, sem):
    del xs_in_ref

    def start(t, c):
        _tile_copy(h_ref, t, xs_ref, dest_ref[0, 0, 2 * t], sem).start()
        _tile_copy(h_ref, t, xs_ref, dest_ref[0, 0, 2 * t + 1], sem).start(priority=1)
        return c

    def wait(t, c):
        _tile_copy(h_ref, t, xs_ref, dest_ref[0, 0, 2 * t], sem).wait()
        _tile_copy(h_ref, t, xs_ref, dest_ref[0, 0, 2 * t + 1], sem).wait()
        return c

    lax.fori_loop(0, TOK_BLOCK, start, 0, unroll=8)
    lax.fori_loop(0, TOK_BLOCK, wait, 0, unroll=8)


def _moe_scatter(dest3, h2t, xs0):
    nblk = dest3.shape[0]
    return pl.pallas_call(
        _moe_scatter_kernel,
        out_shape=jax.ShapeDtypeStruct(xs0.shape, xs0.dtype),
        grid=(nblk,),
        in_specs=[pl.BlockSpec((1, 1, 2 * TOK_BLOCK), lambda i: (i, 0, 0), memory_space=pltpu.SMEM),
                  pl.BlockSpec((TOK_BLOCK * SUBL, LANES), lambda i: (i, 0)),
                  pl.BlockSpec(memory_space=pl.ANY)],
        out_specs=pl.BlockSpec(memory_space=pl.ANY),
        scratch_shapes=[pltpu.SemaphoreType.DMA],
        input_output_aliases={2: 0},
        compiler_params=_cparams(("arbitrary",)),
        name="moe_scatter",
    )(dest3, h2t, xs0)


def _moe_expert_kernel(be_ref, x_ref, wg_ref, wu_ref, wd_ref, y_ref):
    del be_ref
    xb = _from_tiles(x_ref, ROUTE_ROWS).astype(BF16)
    gate = jnp.dot(xb, wg_ref[0].astype(BF16), preferred_element_type=F32)
    up = jnp.dot(xb, wu_ref[0].astype(BF16), preferred_element_type=F32)
    hid = (gate * jax.nn.sigmoid(gate) * up).astype(BF16)
    _to_tiles(y_ref, jnp.dot(hid, wd_ref[0].astype(BF16), preferred_element_type=F32), ROUTE_ROWS)


def _moe_experts(block_expert, xs, w_gate, w_up, w_down):
    n_blocks = xs.shape[0] // (ROUTE_ROWS * SUBL)
    D = w_gate.shape[1]
    tiles = pl.BlockSpec((ROUTE_ROWS * SUBL, LANES), lambda i, be: (i, 0))
    return pl.pallas_call(
        _moe_expert_kernel,
        out_shape=jax.ShapeDtypeStruct(xs.shape, F32),
        grid_spec=pltpu.PrefetchScalarGridSpec(
            num_scalar_prefetch=1,
            grid=(n_blocks,),
            in_specs=[tiles,
                      pl.BlockSpec((1, D, EXPERT_FF), lambda i, be: (be[i], 0, 0)),
                      pl.BlockSpec((1, D, EXPERT_FF), lambda i, be: (be[i], 0, 0)),
                      pl.BlockSpec((1, EXPERT_FF, D), lambda i, be: (be[i], 0, 0))],
            out_specs=tiles),
        compiler_params=_cparams(("arbitrary",)),
        name="moe_experts",
    )(block_expert, xs, w_gate, w_up, w_down)


def _moe_combine_kernel(dest_ref, x1_ref, rt_ref, g2_ref, ys_ref, o_ref, buf0, buf1, sem):
    def start(t, c):
        _tile_copy(ys_ref, dest_ref[0, 0, 2 * t], buf0, t, sem).start()
        _tile_copy(ys_ref, dest_ref[0, 0, 2 * t + 1], buf1, t, sem).start(priority=1)
        return c

    def wait(t, c):
        _tile_copy(ys_ref, dest_ref[0, 0, 2 * t], buf0, t, sem).wait()
        _tile_copy(ys_ref, dest_ref[0, 0, 2 * t + 1], buf1, t, sem).wait()
        return c

    lax.fori_loop(0, TOK_BLOCK, start, 0, unroll=8)
    lax.fori_loop(0, TOK_BLOCK, wait, 0, unroll=8)
    rt = rt_ref[...]
    moe = rt[:, 2:3] * _from_tiles(buf0, TOK_BLOCK) + rt[:, 3:4] * _from_tiles(buf1, TOK_BLOCK)
    o_ref[...] = x1_ref[...] + g2_ref[0] * moe


def _moe_combine(dest3, x1, route, mod3, ys, per):
    N, D = x1.shape
    nb = dest3.shape[0]
    return pl.pallas_call(
        _moe_combine_kernel,
        out_shape=jax.ShapeDtypeStruct((N, D), F32),
        grid=(nb,),
        in_specs=[pl.BlockSpec((1, 1, 2 * TOK_BLOCK), lambda i: (i, 0, 0), memory_space=pltpu.SMEM),
                  pl.BlockSpec((TOK_BLOCK, D), lambda i: (i, 0)),
                  pl.BlockSpec((TOK_BLOCK, ROUTER_LANES), lambda i: (i, 0)),
                  pl.BlockSpec((1, 1, D), lambda i: (i // per, 0, 5)),
                  pl.BlockSpec(memory_space=pl.ANY)],
        out_specs=pl.BlockSpec((TOK_BLOCK, D), lambda i: (i, 0)),
        scratch_shapes=[pltpu.VMEM((TOK_BLOCK * SUBL, LANES), F32), pltpu.VMEM((TOK_BLOCK * SUBL, LANES), F32),
                        pltpu.SemaphoreType.DMA],
        compiler_params=_cparams(("arbitrary",)),
        name="moe_combine",
    )(dest3, x1, route, mod3, ys)


def _rope_tables(T):
    rows = T // GRID_W
    row_id = jnp.repeat(jnp.arange(rows), GRID_W).astype(F32)
    col_id = jnp.tile(jnp.arange(GRID_W), rows).astype(F32)
    inv = ROPE_THETA ** (-jnp.arange(0, AXIS_DIM, 2, dtype=F32) / AXIS_DIM)
    ar = row_id[:, None] * inv
    ac = col_id[:, None] * inv
    ang = jnp.concatenate([ar, ar, ac, ac], axis=-1)
    cos = jnp.concatenate([jnp.ones((TOK_BLOCK, HEAD_DIM), F32), jnp.cos(ang)], axis=0)
    sin = jnp.concatenate([jnp.zeros((TOK_BLOCK, HEAD_DIM), F32), jnp.sin(ang)], axis=0)
    return jnp.tile(cos, (1, 2)), jnp.tile(sin, (1, 2))


def _block_ones(n):
    g = jnp.arange(n) // HEAD_DIM
    return (g[:, None] == g[None, :]).astype(BF16)


def kernel(x, c, ctx, c_ctx, norm1_w, norm2_w, w_mod, b_mod, w_in, shift_w, rwkv_w0, rwkv_w_up,
           rwkv_a0, rwkv_a_up, rwkv_g_up, rwkv_k_k, rwkv_k_a, rwkv_r_k, rwkv_ln_w, rwkv_ln_b,
           q_norm_w, k_norm_w, lam_q1, lam_k1, lam_q2, lam_k2, subln_w, w_out, w_group, b_group,
           w_expert, b_expert, moe_w_gate, moe_w_up, moe_w_down):
    B, T, D = x.shape
    assert ctx.shape[1] == TOK_BLOCK and T % ATTN_TQ == 0 and D == D_MODEL
    N = B * T

    mod_rows = (B + 1 + 7) // 8 * 8
    cc = jnp.zeros((mod_rows, D), F32).at[:B].set(c).at[B].set(c_ctx)
    mod = _modulation(cc, w_mod[0], b_mod[0][None])
    mod3 = mod.reshape(mod_rows, 1, N_MOD * D)
    zpad = jnp.zeros((2, 64, RWKV_WIDTH), F32)
    wup_pad = jnp.concatenate([rwkv_w_up[0], zpad], axis=1)
    aup_pad = jnp.concatenate([zpad, rwkv_a_up[0]], axis=1)
    row = lambda a: a.reshape(1, -1)
    qk_w = jnp.concatenate([jnp.tile(q_norm_w[0], DIFF_WIDTH // HEAD_DIM),
                            jnp.tile(k_norm_w[0], DIFF_WIDTH // HEAD_DIM)])[None]
    lam_p = jnp.stack([lam_q1[0], lam_k1[0], lam_q2[0], lam_k2[0]])
    rw = jnp.zeros((D, ROUTER_LANES), F32).at[:, :N_GROUPS].set(w_group[0])
    rw = rw.at[:, N_GROUPS:N_GROUPS + N_EXPERTS].set(w_expert[0])
    rw_hi = rw.astype(BF16)
    rw_hl = jnp.concatenate([rw_hi, (rw - rw_hi.astype(F32)).astype(BF16)], axis=1)
    rb = jnp.zeros((1, ROUTER_LANES), F32).at[0, :N_GROUPS].set(b_group[0])
    rb = rb.at[0, N_GROUPS:N_GROUPS + N_EXPERTS].set(b_expert[0])
    cos, sin = _rope_tables(T)

    pr, pd = _inproj(ctx, x, mod3, row(norm1_w[0]), w_in[0].astype(BF16))

    r, v, kn, lw, kd, bb, g, bonus = _rwkv_prep(
        pr, shift_w[0], rwkv_w0[0][:, None, :], wup_pad, rwkv_a0[0][:, None, :], aup_pad,
        rwkv_g_up[0], row(rwkv_k_k[0]), row(rwkv_k_a[0]), row(rwkv_r_k[0]), _block_ones(RWKV_WIDTH))
    y_f = _rwkv_scan(0, r, v, kn, lw, kd, bb)
    y_b = _rwkv_scan(1, r, v, kn, lw, kd, bb)
    o_r = _rwkv_readout(y_f, y_b, g, bonus, row(rwkv_ln_w[0]), row(rwkv_ln_b[0]), _block_ones(LANES))

    q, k = _qk_prep(pd, qk_w, cos, sin, _block_ones(2 * DIFF_WIDTH))
    o_d = _diff_attn(q, k, pd, lam_p, row(subln_w[0]))

    wo = w_out[0].astype(BF16)
    x1, h2t, route, cnt = _outproj(o_r, o_d, x, wo[:RWKV_WIDTH], wo[RWKV_WIDTH:], mod3,
                             row(norm2_w[0]), rw_hl, rb)

    R = ROUTE_ROWS
    route2 = route.reshape(N, ROUTER_LANES)
    flat_e = route2[:, 0:2].astype(jnp.int32).reshape(2 * N)
    rank = route2[:, 4:6].astype(jnp.int32).reshape(2 * N)
    counts = cnt[0, N_GROUPS:N_GROUPS + N_EXPERTS].astype(jnp.int32)
    padded = (counts + R - 1) // R * R
    pad_end = jnp.cumsum(padded)
    pad_start = pad_end - padded
    dest = (pad_start[flat_e] + rank).astype(jnp.int32)
    n_blocks = (2 * N + N_EXPERTS * (R - 1) + R - 1) // R
    block_start = jnp.arange(n_blocks, dtype=jnp.int32) * R
    block_expert = jnp.minimum(jnp.sum((pad_end[None, :] <= block_start[:, None]).astype(jnp.int32), axis=1),
                               N_EXPERTS - 1).astype(jnp.int32)
    dest3 = dest.reshape(N // TOK_BLOCK, 1, 2 * TOK_BLOCK)

    xs = _moe_scatter(dest3, h2t, jnp.zeros((n_blocks * R * SUBL, LANES), F32))
    ys = _moe_experts(block_expert, xs, moe_w_gate[0], moe_w_up[0], moe_w_down[0])
    out = _moe_combine(dest3, x1.reshape(N, D), route2, mod3, ys, T // TOK_BLOCK)
    return out.reshape(B, T, D)
```

```python
import functools
import math

import jax
import jax.numpy as jnp
from jax import lax
from jax.experimental import pallas as pl
from jax.experimental.pallas import tpu as pltpu

F32 = jnp.float32
BF16 = jnp.bfloat16

D_MODEL = 1024
HEAD_DIM = 64
RWKV_WIDTH = 512
RWKV_HEADS = 8
DIFF_WIDTH = 512
DIFF_HEADS = 4
DIR_LORA = 256
RWKV_IN = 2048
DIFF_IN = 1536
IN_WIDTH = RWKV_IN + DIFF_IN
GRID_W = 64
AXIS_DIM = HEAD_DIM // 2
ROPE_THETA = 10000.0
DIFF_SCALE = HEAD_DIM ** -0.5
N_GROUPS = 4
EXPERTS_PER_GROUP = 8
N_EXPERTS = 32
EXPERT_FF = 512
NORM_EPS = 1e-6
GN_EPS = 64e-5
N_MOD = 6
LAM_INIT = 0.8 - 0.6 * math.exp(-0.3 * 0)

TOK_BLOCK = 256
CHUNK = 64
CHUNKS_PER_BLOCK = TOK_BLOCK // CHUNK
HEAD_PAIRS = RWKV_HEADS // 2
SCAN_BATCH = 2
LANES = 128
SUBL = 8
ROUTE_ROWS = 512
ROUTER_LANES = 128
ATTN_TQ = 2048
ATTN_SUB = 64
LOG2E = math.log2(math.e)
VMEM_LIMIT = 56 * 1024 * 1024


def _cparams(sem):
    return pltpu.CompilerParams(dimension_semantics=sem, vmem_limit_bytes=VMEM_LIMIT)


def _mod_kernel(c_ref, w_ref, b_ref, o_ref):
    c = c_ref[...]
    s = c * jax.nn.sigmoid(c)
    o_ref[...] = jnp.dot(s, w_ref[...], preferred_element_type=F32,
                         precision=lax.Precision.HIGHEST) + b_ref[...]


def _modulation(cc, w_mod, b_mod):
    rows = cc.shape[0]
    ncol = w_mod.shape[1] // D_MODEL
    return pl.pallas_call(
        _mod_kernel,
        out_shape=jax.ShapeDtypeStruct((rows, w_mod.shape[1]), F32),
        grid=(ncol,),
        in_specs=[pl.BlockSpec((rows, D_MODEL), lambda j: (0, 0)),
                  pl.BlockSpec((D_MODEL, D_MODEL), lambda j: (0, j)),
                  pl.BlockSpec((1, D_MODEL), lambda j: (0, j))],
        out_specs=pl.BlockSpec((rows, D_MODEL), lambda j: (0, j)),
        compiler_params=_cparams(("arbitrary",)),
        name="modulation",
    )(cc, w_mod, b_mod)


def _inproj_kernel(ctx_ref, x_ref, sh_ref, sc_ref, nw_ref, w_ref, qkw_ref, cos_ref, sin_ref, g1_ref, g2_ref,
                   pr_ref, q_ref, k_ref, v_ref):
    i = pl.program_id(1)
    xin = jnp.where(i == 0, ctx_ref[0], x_ref[0])
    ms = jnp.mean(xin * xin, axis=-1, keepdims=True)
    y = xin * lax.rsqrt(ms + NORM_EPS) * nw_ref[...]
    h = (y * (1.0 + sc_ref[0]) + sh_ref[0]).astype(BF16)
    pd = jnp.dot(h, w_ref[:, RWKV_IN:], preferred_element_type=F32)
    pr_ref[0] = jnp.dot(h, w_ref[:, :RWKV_IN], preferred_element_type=F32).astype(BF16)
    v_ref[0] = pd[:, 2 * DIFF_WIDTH:].astype(BF16)
    z = pd[:, :2 * DIFF_WIDTH]
    ss = jnp.dot((z * z).astype(BF16), g1_ref[...], preferred_element_type=F32)
    ss_hi = ss.astype(BF16)
    ss_lo = (ss - ss_hi.astype(F32)).astype(BF16)
    msq = (jnp.dot(ss_hi, g2_ref[...], preferred_element_type=F32)
           + jnp.dot(ss_lo, g2_ref[...], preferred_element_type=F32)) * (1.0 / HEAD_DIM)
    zn = z * lax.rsqrt(msq + NORM_EPS) * qkw_ref[...]
    cos = cos_ref[...]
    sin = sin_ref[...]
    lane = lax.broadcasted_iota(jnp.int32, (1, LANES), 1)
    first = (lane % (2 * 16)) < 16
    nq = DIFF_WIDTH // LANES
    for j in range(2 * nq):
        t = zn[:, j * LANES:(j + 1) * LANES]
        rot = jnp.where(first, -pltpu.roll(t, LANES - 16, 1), pltpu.roll(t, 16, 1))
        out = t * cos + rot * sin
        if j < nq:
            q_ref[0, :, j * LANES:(j + 1) * LANES] = (out * (DIFF_SCALE * LOG2E)).astype(BF16)
        else:
            k_ref[0, :, (j - nq) * LANES:(j - nq + 1) * LANES] = out.astype(BF16)


def _mod_row(nb):
    return lambda b, i: jnp.where(i == 0, nb, b)


def _inproj(ctx, x, mod3, norm1_w, w_in_bf, qk_w, cos, sin, g1, g2):
    B, T, D = x.shape
    nblk = 1 + T // TOK_BLOCK
    S = nblk * TOK_BLOCK
    row = _mod_row(B)
    full = lambda a: pl.BlockSpec(a.shape, lambda b, i: (0,) * a.ndim)
    tok = lambda w: pl.BlockSpec((1, TOK_BLOCK, w), lambda b, i: (b, i, 0))
    return pl.pallas_call(
        _inproj_kernel,
        out_shape=(jax.ShapeDtypeStruct((B, S, RWKV_IN), BF16),
                   jax.ShapeDtypeStruct((B, T, DIFF_WIDTH), BF16),
                   jax.ShapeDtypeStruct((B, S, DIFF_WIDTH), BF16),
                   jax.ShapeDtypeStruct((B, S, DIFF_WIDTH), BF16)),
        grid=(B, nblk),
        in_specs=[pl.BlockSpec((1, TOK_BLOCK, D), lambda b, i: (b, 0, 0)),
                  pl.BlockSpec((1, TOK_BLOCK, D), lambda b, i: (b, jnp.maximum(i - 1, 0), 0)),
                  pl.BlockSpec((1, 1, D), lambda b, i: (row(b, i), 0, 0)),
                  pl.BlockSpec((1, 1, D), lambda b, i: (row(b, i), 0, 1)),
                  full(norm1_w), full(w_in_bf), full(qk_w),
                  pl.BlockSpec((TOK_BLOCK, LANES), lambda b, i: (i, 0)),
                  pl.BlockSpec((TOK_BLOCK, LANES), lambda b, i: (i, 0)),
                  full(g1), full(g2)],
        out_specs=(tok(RWKV_IN),
                   pl.BlockSpec((1, TOK_BLOCK, DIFF_WIDTH), lambda b, i: (b, jnp.maximum(i - 1, 0), 0)),
                   tok(DIFF_WIDTH), tok(DIFF_WIDTH)),
        compiler_params=_cparams(("parallel", "arbitrary")),
        name="inproj",
    )(ctx, x, mod3, mod3, norm1_w, w_in_bf, qk_w, cos, sin, g1, g2)


def _head_sum(v, e_ref):
    return jnp.dot(v, e_ref[...], preferred_element_type=F32)


def _rwkv_prep_kernel(nblk, z_ref, zp_ref, zn_ref, sw_ref, w0_ref, wup_ref, a0_ref, aup_ref,
                      gup_ref, kk_ref, ka_ref, rk_ref, e_ref,
                      r_ref, v_ref, kn_ref, lw_ref, kd_ref, bb_ref, g_ref, bo_ref):
    i = pl.program_id(1)
    z = z_ref[0].astype(F32)
    row = lax.broadcasted_iota(jnp.int32, (TOK_BLOCK, 1), 0)
    prev_ok = jnp.logical_and(i != 0, i != 1)
    next_ok = jnp.logical_and(i != 0, i != nblk - 1)
    zp = zp_ref[0][15:16, :].astype(F32) * prev_ok.astype(F32)
    zn = zn_ref[0][0:1, :].astype(F32) * next_ok.astype(F32)
    prev = jnp.where(row == 0, zp, pltpu.roll(z, 1, 0))
    nxt = jnp.where(row == TOK_BLOCK - 1, zn, pltpu.roll(z, TOK_BLOCK - 1, 0))
    sw = sw_ref[...]
    rx = sw[0:1] * prev + sw[1:2] * z + sw[2:3] * nxt
    r = rx[:, 0:512]
    k = rx[:, 512:1024]
    v = rx[:, 1024:1536]
    kkf = k * kk_ref[...]
    nrm = jnp.sqrt(_head_sum(kkf * kkf, e_ref))
    kn = kkf / jnp.maximum(nrm, 1e-12)
    for p in range(HEAD_PAIRS):
        sl = slice(p * LANES, (p + 1) * LANES)
        r_ref[0, p] = r[:, sl].astype(BF16)
        v_ref[0, p] = v[:, sl].astype(BF16)
        kn_ref[0, p] = kn[:, sl].astype(BF16)
    for d in range(2):
        base = 1536 + d * DIR_LORA
        x128 = rx[:, base:base + 128]
        lg = rx[:, base + 128:base + 256]
        w_raw = w0_ref[d] + jnp.dot(jnp.tanh(x128), wup_ref[d], preferred_element_type=F32)
        logw = -math.exp(-0.5) / (1.0 + jnp.exp(-w_raw))
        a = jax.nn.sigmoid(a0_ref[d] + jnp.dot(x128, aup_ref[d], preferred_element_type=F32))
        g = jnp.dot(jax.nn.sigmoid(lg), gup_ref[d], preferred_element_type=F32)
        kd = k * (1.0 + (a - 1.0) * ka_ref[...])
        bonus = _head_sum(r * kd * rk_ref[...], e_ref) * v
        bb = kn * a
        for p in range(HEAD_PAIRS):
            sl = slice(p * LANES, (p + 1) * LANES)
            lw_ref[d, 0, p] = logw[:, sl]
            kd_ref[d, 0, p] = kd[:, sl].astype(BF16)
            bb_ref[d, 0, p] = bb[:, sl].astype(BF16)
        g_ref[d, 0] = g.astype(BF16)
        bo_ref[d, 0] = bonus.astype(BF16)


def _rwkv_prep(pr, shift_w, w0, wup_pad, a0, aup_pad, g_up, k_k, k_a, r_k, e512):
    B, S, _ = pr.shape
    nblk = S // TOK_BLOCK
    hb = TOK_BLOCK // 16
    nh = S // 16
    pair = lambda dt: jax.ShapeDtypeStruct((B, HEAD_PAIRS, S, LANES), dt)
    pair2 = lambda dt: jax.ShapeDtypeStruct((2, B, HEAD_PAIRS, S, LANES), dt)
    chan2 = jax.ShapeDtypeStruct((2, B, S, RWKV_WIDTH), BF16)
    pspec = pl.BlockSpec((1, HEAD_PAIRS, TOK_BLOCK, LANES), lambda b, i: (b, 0, i, 0))
    pspec2 = pl.BlockSpec((2, 1, HEAD_PAIRS, TOK_BLOCK, LANES), lambda b, i: (0, b, 0, i, 0))
    cspec2 = pl.BlockSpec((2, 1, TOK_BLOCK, RWKV_WIDTH), lambda b, i: (0, b, i, 0))
    full = lambda a: pl.BlockSpec(a.shape, lambda b, i: (0,) * a.ndim)
    return pl.pallas_call(
        functools.partial(_rwkv_prep_kernel, nblk),
        out_shape=(pair(BF16), pair(BF16), pair(BF16), pair2(F32), pair2(BF16), pair2(BF16),
                   chan2, chan2),
        grid=(B, nblk),
        in_specs=[pl.BlockSpec((1, TOK_BLOCK, RWKV_IN), lambda b, i: (b, i, 0)),
                  pl.BlockSpec((1, 16, RWKV_IN), lambda b, i: (b, jnp.maximum(i * hb - 1, 0), 0)),
                  pl.BlockSpec((1, 16, RWKV_IN), lambda b, i: (b, jnp.minimum((i + 1) * hb, nh - 1), 0)),
                  full(shift_w), full(w0), full(wup_pad), full(a0), full(aup_pad), full(g_up),
                  full(k_k), full(k_a), full(r_k), full(e512)],
        out_specs=(pspec, pspec, pspec, pspec2, pspec2, pspec2, cspec2, cspec2),
        compiler_params=_cparams(("parallel", "arbitrary")),
        name="rwkv_prep",
    )(pr, pr, pr, shift_w, w0, wup_pad, a0, aup_pad, g_up, k_k, k_a, r_k, e512)


def _bmm(a, b):
    return jnp.einsum('nij,njk->nik', a.astype(BF16), b.astype(BF16), preferred_element_type=F32)


def _bmm_nt(a, b):
    return jnp.einsum('nij,nkj->nik', a.astype(BF16), b.astype(BF16), preferred_element_type=F32)


def _rwkv_scan_kernel(reverse, r_ref, v_ref, kn_ref, lw_ref, kd_ref, bb_ref, y_ref,
                      h_ref, q_s, y0_s, m_s, n_s, pt_s):
    C = CHUNK
    NC = CHUNKS_PER_BLOCK

    @pl.when(pl.program_id(1) == 0)
    def _():
        h_ref[...] = jnp.zeros_like(h_ref)

    t_i = lax.broadcasted_iota(jnp.int32, (C, C), 0)
    s_i = lax.broadcasted_iota(jnp.int32, (C, C), 1)
    if reverse:
        strict = s_i > t_i
        incl = s_i >= t_i
    else:
        strict = s_i < t_i
        incl = s_i <= t_i
    blk16 = (t_i // 16) == (s_i // 16)
    blk32 = (t_i // 32) == (s_i // 32)
    eye = (t_i == s_i).astype(F32)
    G = SCAN_BATCH * HEAD_PAIRS
    NB = G * NC
    tri = jnp.broadcast_to(incl.astype(F32)[None], (NB, C, C))
    lane = lax.broadcasted_iota(jnp.int32, (1, 1, LANES), 2)
    head0 = lane < HEAD_DIM
    rr = lax.broadcasted_iota(jnp.int32, (LANES, LANES), 0)
    cc = lax.broadcasted_iota(jnp.int32, (LANES, LANES), 1)
    same_head = ((rr // HEAD_DIM) == (cc // HEAD_DIM))[None]

    def chunked(blk):
        return blk.astype(F32).reshape(NB, C, LANES)

    r = chunked(r_ref[...])
    v = chunked(v_ref[...])
    kn = chunked(kn_ref[...])
    kd = chunked(kd_ref[0])
    bb = chunked(bb_ref[0])
    logw = lw_ref[0].reshape(NB, C, LANES)
    lw_hi = logw.astype(BF16).astype(F32)
    lw_lo = logw - lw_hi
    L = _bmm(tri, lw_hi) + _bmm(tri, lw_lo)
    Lex = L - logw
    tot = L[:, 0:1, :] if reverse else L[:, C - 1:C, :]
    e_in = jnp.exp(-L)
    e_hat = jnp.exp(tot - L)
    at = -kn * jnp.exp(Lex)
    rt = r * jnp.exp(L)
    bt = bb * e_in
    kt = kd * e_in
    bh = bb * e_hat
    kh = kd * e_hat
    lhs = jnp.concatenate([at, rt], axis=1)
    rhs = jnp.concatenate([bt, kt], axis=1)
    sc = jnp.concatenate([_bmm_nt(jnp.where(head0, lhs, 0.0), rhs),
                          _bmm_nt(jnp.where(head0, 0.0, lhs), rhs)], axis=0)
    a_ab = jnp.where(strict, sc[:, :C, :C], 0.0)
    a_ak = jnp.where(strict, sc[:, :C, C:], 0.0)
    a_rb = jnp.where(incl, sc[:, C:, :C], 0.0)
    a_rk = jnp.where(incl, sc[:, C:, C:], 0.0)
    nd = jnp.where(blk16, a_ab, 0.0)
    T = eye + nd
    pw = _bmm(nd, nd)
    T = T + _bmm(pw, T)
    pw = _bmm(pw, pw)
    T = T + _bmm(pw, T)
    pw = _bmm(pw, pw)
    T = T + _bmm(pw, T)
    off = jnp.where(jnp.logical_and(blk32, jnp.logical_not(blk16)), a_ab, 0.0)
    T = T + _bmm(T, _bmm(off, T))
    off = jnp.where(blk32, 0.0, a_ab)
    T = T + _bmm(T, _bmm(off, T))
    v2 = jnp.concatenate([v, v], axis=0)
    at2 = jnp.concatenate([at, at], axis=0)
    akv = _bmm(a_ak, v2)
    arkv = _bmm(a_rk, v2)
    wu = _bmm(T, jnp.concatenate([at2, akv], axis=2))
    qy = _bmm(a_rb, wu)
    merge = lambda zz: jnp.where(head0, zz[:NB], zz[NB:])
    wup = jnp.concatenate([merge(wu[:, :, :LANES]), merge(wu[:, :, LANES:])], axis=2)
    q_s[...] = (rt + merge(qy[:, :, :LANES])).reshape(G, NC, C, LANES)
    y0_s[...] = merge(qy[:, :, LANES:] + arkv).reshape(G, NC, C, LANES)
    bht = jnp.swapaxes(bh, 1, 2)
    kht = jnp.swapaxes(kh, 1, 2)
    mn = _bmm(bht, wup)
    m_s[...] = jnp.where(same_head, mn[:, :, :LANES], 0.0).reshape(G, NC, LANES, LANES)
    n_s[...] = jnp.where(same_head, mn[:, :, LANES:] + _bmm(kht, v), 0.0).reshape(G, NC, LANES, LANES)
    pt_s[...] = jnp.exp(jnp.sum(jnp.swapaxes(logw, 1, 2), axis=2, keepdims=True)).reshape(G, NC, LANES, 1)
    for j in range(NC):
        c = NC - 1 - j if reverse else j
        H = h_ref[...]
        y = _bmm(q_s[:, c], H) + y0_s[:, c]
        y_ref[:, :, c * C:(c + 1) * C, :] = y.reshape(SCAN_BATCH, HEAD_PAIRS, C, LANES)
        h_ref[...] = pt_s[:, c] * H + _bmm(m_s[:, c], H) + n_s[:, c]


def _rwkv_scan(d, r, v, kn, lw, kd, bb):
    B, _, S, _ = r.shape
    nblk = S // TOK_BLOCK
    reverse = d == 1
    if reverse:
        blk = lambda i: jnp.where(i == 0, 0, nblk - i)
    else:
        blk = lambda i: i
    SB = SCAN_BATCH
    pspec = pl.BlockSpec((SB, HEAD_PAIRS, TOK_BLOCK, LANES), lambda b, i: (b, 0, blk(i), 0))
    pspec_d = pl.BlockSpec((1, SB, HEAD_PAIRS, TOK_BLOCK, LANES), lambda b, i: (d, b, 0, blk(i), 0))
    NC, C, G = CHUNKS_PER_BLOCK, CHUNK, SB * HEAD_PAIRS
    return pl.pallas_call(
        functools.partial(_rwkv_scan_kernel, reverse),
        out_shape=jax.ShapeDtypeStruct((B, HEAD_PAIRS, S, LANES), F32),
        grid=(B // SB, nblk),
        in_specs=[pspec, pspec, pspec, pspec_d, pspec_d, pspec_d],
        out_specs=pspec,
        scratch_shapes=[pltpu.VMEM((G, LANES, LANES), F32),
                        pltpu.VMEM((G, NC, C, LANES), F32),
                        pltpu.VMEM((G, NC, C, LANES), F32),
                        pltpu.VMEM((G, NC, LANES, LANES), F32),
                        pltpu.VMEM((G, NC, LANES, LANES), F32),
                        pltpu.VMEM((G, NC, LANES, 1), F32)],
        compiler_params=_cparams(("parallel", "arbitrary")),
        name="rwkv_scan_rev" if reverse else "rwkv_scan_fwd",
    )(r, v, kn, lw, kd, bb)


def _rwkv_readout_block(y0_ref, y1_ref, g_ref, bo_ref, lnw_ref, lnb_ref, e_ref):
    inv = 1.0 / HEAD_DIM
    cols = []
    for p in range(HEAD_PAIRS):
        sl = slice(p * LANES, (p + 1) * LANES)
        acc = None
        for d, y_ref in enumerate((y0_ref, y1_ref)):
            y = y_ref[0, p]
            mu = _head_sum(y, e_ref) * inv
            dl = y - mu
            var = _head_sum(dl * dl, e_ref) * inv
            yn = dl * lax.rsqrt(var + GN_EPS) * lnw_ref[:, sl] + lnb_ref[:, sl]
            o = (yn + bo_ref[d, 0, :, sl].astype(F32)) * g_ref[d, 0, :, sl].astype(F32)
            acc = o if acc is None else acc + o
        cols.append(acc)
    return jnp.concatenate(cols, axis=1)


def _diff_attn_kernel(nkv, q_ref, k_ref, v_ref, lam_ref, sw_ref, o_ref,
                      qs_ref, s0_ref, s1_ref, p0_ref, p1_ref, al0_ref, al1_ref, m_ref, l_ref, acc_ref):
    TQ = ATTN_TQ
    TK = TOK_BLOCK
    q = q_ref[0]
    lane = lax.broadcasted_iota(jnp.int32, (1, LANES), 1)
    zero = jnp.zeros_like(q)
    qs_ref[0:TQ, :] = jnp.where(lane < HEAD_DIM, q, zero)
    qs_ref[TQ:2 * TQ, :] = jnp.where(lane < HEAD_DIM, zero, q)
    m_ref[...] = jnp.full(m_ref.shape, -1e30, F32)
    l_ref[...] = jnp.zeros(l_ref.shape, F32)
    acc_ref[...] = jnp.zeros(acc_ref.shape, F32)
    p1_ref[...] = jnp.zeros(p1_ref.shape, BF16)
    al1_ref[...] = jnp.ones(al1_ref.shape, F32)

    def chunk(ref, j):
        return ref[0, pl.ds(pl.multiple_of(j * TK, TK), TK), :]

    def scores(j, s_ref):
        s_ref[...] = lax.dot_general(qs_ref[...], chunk(k_ref, j), (((1,), (1,)), ((), ())),
                                     preferred_element_type=F32)

    def softmax_step(s_ref, p_ref, al_ref):
        for r in range(2 * TQ // ATTN_SUB):
            rows = slice(r * ATTN_SUB, (r + 1) * ATTN_SUB)
            s = s_ref[rows, :]
            m_old = m_ref[rows, :]
            mx = jnp.maximum(s[:, :LANES], s[:, LANES:])
            m_new = jnp.maximum(m_old, jnp.max(mx, axis=1, keepdims=True))
            al = jnp.exp2(m_old - m_new)
            p = jnp.exp2(s - jnp.concatenate([m_new, m_new], axis=1))
            l_ref[rows, :] = al * l_ref[rows, :] + (p[:, :LANES] + p[:, LANES:])
            m_ref[rows, :] = m_new
            al_ref[rows, :] = al
            p_ref[rows, :] = p.astype(BF16)

    def values(j, p_ref, al_ref):
        acc_ref[...] = al_ref[...] * acc_ref[...] + jnp.dot(p_ref[...], chunk(v_ref, j),
                                                             preferred_element_type=F32)

    scores(0, s0_ref)

    def pair(t, carry):
        j = 2 * t
        scores(j + 1, s1_ref)
        softmax_step(s0_ref, p0_ref, al0_ref)
        values(jnp.maximum(j - 1, 0), p1_ref, al1_ref)
        scores(jnp.minimum(j + 2, nkv - 1), s0_ref)
        softmax_step(s1_ref, p1_ref, al1_ref)
        values(j, p0_ref, al0_ref)
        return carry

    lax.fori_loop(0, nkv // 2, pair, 0)
    if nkv % 2:
        softmax_step(s0_ref, p0_ref, al0_ref)
        if nkv > 1:
            values(nkv - 2, p1_ref, al1_ref)
        values(nkv - 1, p0_ref, al0_ref)
    else:
        values(nkv - 1, p1_ref, al1_ref)

    o = acc_ref[...] / jnp.sum(l_ref[...], axis=1, keepdims=True)
    lp = lam_ref[...]
    lam = (jnp.exp(jnp.sum(lp[0:1] * lp[1:2], axis=1, keepdims=True))
           - jnp.exp(jnp.sum(lp[2:3] * lp[3:4], axis=1, keepdims=True)) + LAM_INIT)
    od = o[:TQ] - lam * o[TQ:]
    ms = jnp.mean(od * od, axis=-1, keepdims=True)
    o_ref[0] = (od * lax.rsqrt(ms + NORM_EPS) * sw_ref[...] * (1.0 - LAM_INIT)).astype(BF16)


def _diff_attn(q, k, v, lam_p, subln_w):
    B, S, _ = k.shape
    T = S - TOK_BLOCK
    H = DIFF_HEADS
    TQ = ATTN_TQ
    TQ2 = 2 * TQ
    return pl.pallas_call(
        functools.partial(_diff_attn_kernel, S // TOK_BLOCK),
        out_shape=jax.ShapeDtypeStruct((B, T, DIFF_WIDTH), BF16),
        grid=(B, H, T // TQ),
        in_specs=[pl.BlockSpec((1, TQ, LANES), lambda b, h, i: (b, i, h)),
                  pl.BlockSpec((1, S, LANES), lambda b, h, i: (b, 0, h)),
                  pl.BlockSpec((1, S, LANES), lambda b, h, i: (b, 0, h)),
                  pl.BlockSpec(lam_p.shape, lambda b, h, i: (0, 0)),
                  pl.BlockSpec(subln_w.shape, lambda b, h, i: (0, 0))],
        out_specs=pl.BlockSpec((1, TQ, LANES), lambda b, h, i: (b, i, h)),
        scratch_shapes=[pltpu.VMEM((TQ2, LANES), BF16),
                        pltpu.VMEM((TQ2, 2 * LANES), F32), pltpu.VMEM((TQ2, 2 * LANES), F32),
                        pltpu.VMEM((TQ2, 2 * LANES), BF16), pltpu.VMEM((TQ2, 2 * LANES), BF16),
                        pltpu.VMEM((TQ2, LANES), F32), pltpu.VMEM((TQ2, LANES), F32),
                        pltpu.VMEM((TQ2, LANES), F32), pltpu.VMEM((TQ2, LANES), F32),
                        pltpu.VMEM((TQ2, LANES), F32)],
        compiler_params=_cparams(("parallel", "parallel", "arbitrary")),
        name="diff_attn",
    )(q, k, v, lam_p, subln_w)


def _outproj_kernel(y0_ref, y1_ref, g_ref, bo_ref, lnw_ref, lnb_ref, e_ref, od_ref, x_ref, wr_ref, wd_ref,
                    g1_ref, sh_ref, sc_ref, nw_ref, rw_ref, rb_ref, x1_ref, h2_ref, rt_ref, cnt_ref, run_ref):
    o_r = _rwkv_readout_block(y0_ref, y1_ref, g_ref, bo_ref, lnw_ref, lnb_ref, e_ref).astype(BF16)
    attn = (jnp.dot(od_ref[0], wd_ref[...], preferred_element_type=F32)
            + jnp.dot(o_r, wr_ref[...], preferred_element_type=F32))
    x1 = x_ref[0] + g1_ref[0] * attn
    x1_ref[0] = x1
    ms = jnp.mean(x1 * x1, axis=-1, keepdims=True)
    h2 = x1 * lax.rsqrt(ms + NORM_EPS) * nw_ref[...] * (1.0 + sc_ref[0]) + sh_ref[0]
    _to_tiles(h2_ref, h2, TOK_BLOCK)
    hh = h2.astype(BF16)
    hl = (h2 - hh.astype(F32)).astype(BF16)
    hw = jnp.dot(hh, rw_ref[...], preferred_element_type=F32)
    logits = (hw[:, :ROUTER_LANES] + hw[:, ROUTER_LANES:]
              + jnp.dot(hl, rw_ref[:, :ROUTER_LANES], preferred_element_type=F32)) + rb_ref[...]
    lane = lax.broadcasted_iota(jnp.int32, logits.shape, 1).astype(F32)
    neg = -jnp.inf
    big = 1e9
    gl = jnp.where(lane < N_GROUPS, logits, neg)
    gmax = jnp.max(gl, axis=1, keepdims=True)
    gsel = jnp.min(jnp.where(gl == gmax, lane, big), axis=1, keepdims=True)
    gprob = 1.0 / jnp.sum(jnp.exp(gl - gmax), axis=1, keepdims=True)
    lo = N_GROUPS + EXPERTS_PER_GROUP * gsel
    el = jnp.where(jnp.logical_and(lane >= lo, lane < lo + EXPERTS_PER_GROUP), logits, neg)
    v1 = jnp.max(el, axis=1, keepdims=True)
    i1 = jnp.min(jnp.where(el == v1, lane, big), axis=1, keepdims=True)
    el2 = jnp.where(lane == i1, neg, el)
    v2 = jnp.max(el2, axis=1, keepdims=True)
    i2 = jnp.min(jnp.where(el2 == v2, lane, big), axis=1, keepdims=True)
    e21 = jnp.exp(v2 - v1)
    w1 = gprob / (1.0 + e21)
    w2 = gprob * e21 / (1.0 + e21)
    @pl.when(jnp.logical_and(pl.program_id(0) == 0, pl.program_id(1) == 0))
    def _():
        run_ref[...] = jnp.zeros_like(run_ref)

    oh1 = (lane == i1).astype(BF16)
    oh2 = (lane == i2).astype(BF16)
    tr = lax.broadcasted_iota(jnp.int32, (TOK_BLOCK, TOK_BLOCK), 0)
    tc = lax.broadcasted_iota(jnp.int32, (TOK_BLOCK, TOK_BLOCK), 1)
    before = (tc < tr).astype(BF16)
    run = run_ref[...]
    tot1 = jnp.sum(oh1.astype(F32), axis=0, keepdims=True)
    tot2 = jnp.sum(oh2.astype(F32), axis=0, keepdims=True)
    c1 = jnp.dot(before, oh1, preferred_element_type=F32) + run
    c2 = jnp.dot(before, oh2, preferred_element_type=F32) + (run + tot1)
    rank1 = jnp.sum(jnp.where(lane == i1, c1, 0.0), axis=1, keepdims=True)
    rank2 = jnp.sum(jnp.where(lane == i2, c2, 0.0), axis=1, keepdims=True)
    run = run + tot1 + tot2
    run_ref[...] = run
    cnt_ref[...] = jnp.broadcast_to(run, cnt_ref.shape)
    vals = (i1 - N_GROUPS, i2 - N_GROUPS, w1, w2, rank1, rank2)
    out = jnp.zeros_like(logits)
    for n, val in enumerate(vals):
        out = jnp.where(lane == n, val, out)
    rt_ref[0] = out


def _outproj(y0, y1, g, bonus, ln_w, ln_b, e128, o_d, x, w_out_r, w_out_d, mod3, norm2_w, rw_hl, rb):
    B, T, D = x.shape
    yspec = pl.BlockSpec((1, HEAD_PAIRS, TOK_BLOCK, LANES), lambda b, i: (b, 0, i + 1, 0))
    cspec2 = pl.BlockSpec((2, 1, TOK_BLOCK, RWKV_WIDTH), lambda b, i: (0, b, i + 1, 0))
    full = lambda a: pl.BlockSpec(a.shape, lambda b, i: (0,) * a.ndim)
    modspec = lambda col: pl.BlockSpec((1, 1, D), lambda b, i: (b, 0, col))
    tok = lambda w: pl.BlockSpec((1, TOK_BLOCK, w), lambda b, i: (b, i, 0))
    return pl.pallas_call(
        _outproj_kernel,
        out_shape=(jax.ShapeDtypeStruct((B, T, D), F32),
                   jax.ShapeDtypeStruct((B * T * SUBL, LANES), F32),
                   jax.ShapeDtypeStruct((B, T, ROUTER_LANES), F32),
                   jax.ShapeDtypeStruct((8, ROUTER_LANES), F32)),
        grid=(B, T // TOK_BLOCK),
        in_specs=[yspec, yspec, cspec2, cspec2, full(ln_w), full(ln_b), full(e128),
                  tok(DIFF_WIDTH), tok(D), full(w_out_r), full(w_out_d),
                  modspec(2), modspec(3), modspec(4), full(norm2_w),
                  full(rw_hl), full(rb)],
        out_specs=(tok(D), pl.BlockSpec((TOK_BLOCK * SUBL, LANES), lambda b, i: (b * (T // TOK_BLOCK) + i, 0)),
                   tok(ROUTER_LANES), pl.BlockSpec((8, ROUTER_LANES), lambda b, i: (0, 0))),
        scratch_shapes=[pltpu.VMEM((1, ROUTER_LANES), F32)],
        compiler_params=_cparams(("arbitrary", "arbitrary")),
        name="outproj_router",
    )(y0, y1, g, bonus, ln_w, ln_b, e128, o_d, x, w_out_r, w_out_d, mod3, mod3, mod3, norm2_w, rw_hl, rb)


def _tile_copy(src_ref, src_tok, dst_ref, dst_tok, sem):
    return pltpu.make_async_copy(src_ref.at[pl.ds(pl.multiple_of(src_tok * SUBL, SUBL), SUBL), :],
                                 dst_ref.at[pl.ds(pl.multiple_of(dst_tok * SUBL, SUBL), SUBL), :], sem)


def _from_tiles(ref, n):
    return jnp.concatenate([ref[pl.ds(j, n, stride=SUBL), :] for j in range(SUBL)], axis=1)


def _to_tiles(ref, val, n):
    for j in range(SUBL):
        ref[pl.ds(j, n, stride=SUBL), :] = val[:, j * LANES:(j + 1) * LANES]


def _moe_scatter_kernel(dest_ref, h_ref, xs_in_ref, xs_ref, sem):
    del xs_in_ref

    def start(t, c):
        _tile_copy(h_ref, t, xs_ref, dest_ref[0, 0, 2 * t], sem).start()
        _tile_copy(h_ref, t, xs_ref, dest_ref[0, 0, 2 * t + 1], sem).start(priority=1)
        return c

    def wait(t, c):
        _tile_copy(h_ref, t, xs_ref, dest_ref[0, 0, 2 * t], sem).wait()
        _tile_copy(h_ref, t, xs_ref, dest_ref[0, 0, 2 * t + 1], sem).wait()
        return c

    lax.fori_loop(0, TOK_BLOCK, start, 0, unroll=8)
    lax.fori_loop(0, TOK_BLOCK, wait, 0, unroll=8)


def _moe_scatter(dest3, h2t, xs0):
    nblk = dest3.shape[0]
    return pl.pallas_call(
        _moe_scatter_kernel,
        out_shape=jax.ShapeDtypeStruct(xs0.shape, xs0.dtype),
        grid=(nblk,),
        in_specs=[pl.BlockSpec((1, 1, 2 * TOK_BLOCK), lambda i: (i, 0, 0), memory_space=pltpu.SMEM),
                  pl.BlockSpec((TOK_BLOCK * SUBL, LANES), lambda i: (i, 0)),
                  pl.BlockSpec(memory_space=pl.ANY)],
        out_specs=pl.BlockSpec(memory_space=pl.ANY),
        scratch_shapes=[pltpu.SemaphoreType.DMA],
        input_output_aliases={2: 0},
        compiler_params=_cparams(("arbitrary",)),
        name="moe_scatter",
    )(dest3, h2t, xs0)


def _moe_expert_kernel(be_ref, x_ref, wg_ref, wu_ref, wd_ref, y_ref, wg_bf, wu_bf, wd_bf):
    i = pl.program_id(0)

    @pl.when(jnp.logical_or(i == 0, be_ref[i] != be_ref[jnp.maximum(i - 1, 0)]))
    def _():
        wg_bf[...] = wg_ref[0].astype(BF16)
        wu_bf[...] = wu_ref[0].astype(BF16)
        wd_bf[...] = wd_ref[0].astype(BF16)

    xb = _from_tiles(x_ref, ROUTE_ROWS).astype(BF16)
    gate = jnp.dot(xb, wg_bf[...], preferred_element_type=F32)
    up = jnp.dot(xb, wu_bf[...], preferred_element_type=F32)
    hid = (gate * jax.nn.sigmoid(gate) * up).astype(BF16)
    _to_tiles(y_ref, jnp.dot(hid, wd_bf[...], preferred_element_type=F32), ROUTE_ROWS)


def _moe_experts(block_expert, xs, w_gate, w_up, w_down):
    n_blocks = xs.shape[0] // (ROUTE_ROWS * SUBL)
    D = w_gate.shape[1]
    tiles = pl.BlockSpec((ROUTE_ROWS * SUBL, LANES), lambda i, be: (i, 0))
    return pl.pallas_call(
        _moe_expert_kernel,
        out_shape=jax.ShapeDtypeStruct(xs.shape, F32),
        grid_spec=pltpu.PrefetchScalarGridSpec(
            num_scalar_prefetch=1,
            grid=(n_blocks,),
            in_specs=[tiles,
                      pl.BlockSpec((1, D, EXPERT_FF), lambda i, be: (be[i], 0, 0)),
                      pl.BlockSpec((1, D, EXPERT_FF), lambda i, be: (be[i], 0, 0)),
                      pl.BlockSpec((1, EXPERT_FF, D), lambda i, be: (be[i], 0, 0))],
            out_specs=tiles,
            scratch_shapes=[pltpu.VMEM((D, EXPERT_FF), BF16), pltpu.VMEM((D, EXPERT_FF), BF16),
                            pltpu.VMEM((EXPERT_FF, D), BF16)]),
        compiler_params=_cparams(("arbitrary",)),
        name="moe_experts",
    )(block_expert, xs, w_gate, w_up, w_down)


def _moe_combine_kernel(dest_ref, x1_ref, rt_ref, g2_ref, ys_ref, o_ref, buf0, buf1, sem):
    def start(t, c):
        _tile_copy(ys_ref, dest_ref[0, 0, 2 * t], buf0, t, sem).start()
        _tile_copy(ys_ref, dest_ref[0, 0, 2 * t + 1], buf1, t, sem).start(priority=1)
        return c

    def wait(t, c):
        _tile_copy(ys_ref, dest_ref[0, 0, 2 * t], buf0, t, sem).wait()
        _tile_copy(ys_ref, dest_ref[0, 0, 2 * t + 1], buf1, t, sem).wait()
        return c

    lax.fori_loop(0, TOK_BLOCK, start, 0, unroll=8)
    lax.fori_loop(0, TOK_BLOCK, wait, 0, unroll=8)
    rt = rt_ref[...]
    moe = rt[:, 2:3] * _from_tiles(buf0, TOK_BLOCK) + rt[:, 3:4] * _from_tiles(buf1, TOK_BLOCK)
    o_ref[...] = x1_ref[...] + g2_ref[0] * moe


def _moe_combine(dest3, x1, route, mod3, ys, per):
    N, D = x1.shape
    nb = dest3.shape[0]
    return pl.pallas_call(
        _moe_combine_kernel,
        out_shape=jax.ShapeDtypeStruct((N, D), F32),
        grid=(nb,),
        in_specs=[pl.BlockSpec((1, 1, 2 * TOK_BLOCK), lambda i: (i, 0, 0), memory_space=pltpu.SMEM),
                  pl.BlockSpec((TOK_BLOCK, D), lambda i: (i, 0)),
                  pl.BlockSpec((TOK_BLOCK, ROUTER_LANES), lambda i: (i, 0)),
                  pl.BlockSpec((1, 1, D), lambda i: (i // per, 0, 5)),
                  pl.BlockSpec(memory_space=pl.ANY)],
        out_specs=pl.BlockSpec((TOK_BLOCK, D), lambda i: (i, 0)),
        scratch_shapes=[pltpu.VMEM((TOK_BLOCK * SUBL, LANES), F32), pltpu.VMEM((TOK_BLOCK * SUBL, LANES), F32),
                        pltpu.SemaphoreType.DMA],
        compiler_params=_cparams(("arbitrary",)),
        name="moe_combine",
    )(dest3, x1, route, mod3, ys)


def _rope_tables(T):
    rows = T // GRID_W
    row_id = jnp.repeat(jnp.arange(rows), GRID_W).astype(F32)
    col_id = jnp.tile(jnp.arange(GRID_W), rows).astype(F32)
    inv = ROPE_THETA ** (-jnp.arange(0, AXIS_DIM, 2, dtype=F32) / AXIS_DIM)
    ar = row_id[:, None] * inv
    ac = col_id[:, None] * inv
    ang = jnp.concatenate([ar, ar, ac, ac], axis=-1)
    cos = jnp.concatenate([jnp.ones((TOK_BLOCK, HEAD_DIM), F32), jnp.cos(ang)], axis=0)
    sin = jnp.concatenate([jnp.zeros((TOK_BLOCK, HEAD_DIM), F32), jnp.sin(ang)], axis=0)
    return jnp.tile(cos, (1, 2)), jnp.tile(sin, (1, 2))


def _block_ones(n):
    g = jnp.arange(n) // HEAD_DIM
    return (g[:, None] == g[None, :]).astype(BF16)


def kernel(x, c, ctx, c_ctx, norm1_w, norm2_w, w_mod, b_mod, w_in, shift_w, rwkv_w0, rwkv_w_up,
           rwkv_a0, rwkv_a_up, rwkv_g_up, rwkv_k_k, rwkv_k_a, rwkv_r_k, rwkv_ln_w, rwkv_ln_b,
           q_norm_w, k_norm_w, lam_q1, lam_k1, lam_q2, lam_k2, subln_w, w_out, w_group, b_group,
           w_expert, b_expert, moe_w_gate, moe_w_up, moe_w_down):
    B, T, D = x.shape
    assert ctx.shape[1] == TOK_BLOCK and T % ATTN_TQ == 0 and D == D_MODEL and B % SCAN_BATCH == 0
    N = B * T

    mod_rows = (B + 1 + 7) // 8 * 8
    cc = jnp.zeros((mod_rows, D), F32).at[:B].set(c).at[B].set(c_ctx)
    mod = _modulation(cc, w_mod[0], b_mod[0][None])
    mod3 = mod.reshape(mod_rows, 1, N_MOD * D)
    zpad = jnp.zeros((2, 64, RWKV_WIDTH), F32)
    wup_pad = jnp.concatenate([rwkv_w_up[0], zpad], axis=1)
    aup_pad = jnp.concatenate([zpad, rwkv_a_up[0]], axis=1)
    row = lambda a: a.reshape(1, -1)
    qk_w = jnp.concatenate([jnp.tile(q_norm_w[0], DIFF_WIDTH // HEAD_DIM),
                            jnp.tile(k_norm_w[0], DIFF_WIDTH // HEAD_DIM)])[None]
    lam_p = jnp.stack([lam_q1[0], lam_k1[0], lam_q2[0], lam_k2[0]])
    rw = jnp.zeros((D, ROUTER_LANES), F32).at[:, :N_GROUPS].set(w_group[0])
    rw = rw.at[:, N_GROUPS:N_GROUPS + N_EXPERTS].set(w_expert[0])
    rw_hi = rw.astype(BF16)
    rw_hl = jnp.concatenate([rw_hi, (rw - rw_hi.astype(F32)).astype(BF16)], axis=1)
    rb = jnp.zeros((1, ROUTER_LANES), F32).at[0, :N_GROUPS].set(b_group[0])
    rb = rb.at[0, N_GROUPS:N_GROUPS + N_EXPERTS].set(b_expert[0])
    cos, sin = _rope_tables(T)

    head_of = jnp.arange(2 * DIFF_WIDTH) // HEAD_DIM
    g1 = (head_of[:, None] == jnp.arange(LANES)[None, :]).astype(BF16)
    pr, att_q, att_k, att_v = _inproj(ctx, x, mod3, row(norm1_w[0]), w_in[0].astype(BF16), qk_w, cos, sin, g1, g1.T)

    r, v, kn, lw, kd, bb, g, bonus = _rwkv_prep(
        pr, shift_w[0], rwkv_w0[0][:, None, :], wup_pad, rwkv_a0[0][:, None, :], aup_pad,
        rwkv_g_up[0], row(rwkv_k_k[0]), row(rwkv_k_a[0]), row(rwkv_r_k[0]), _block_ones(RWKV_WIDTH))
    y_f = _rwkv_scan(0, r, v, kn, lw, kd, bb)
    y_b = _rwkv_scan(1, r, v, kn, lw, kd, bb)

    o_d = _diff_attn(att_q, att_k, att_v, lam_p, row(subln_w[0]))

    wo = w_out[0].astype(BF16)
    x1, h2t, route, cnt = _outproj(y_f, y_b, g, bonus, row(rwkv_ln_w[0]), row(rwkv_ln_b[0]), _block_ones(LANES),
                                   o_d, x, wo[:RWKV_WIDTH], wo[RWKV_WIDTH:], mod3, row(norm2_w[0]), rw_hl, rb)

    R = ROUTE_ROWS
    route2 = route.reshape(N, ROUTER_LANES)
    flat_e = route2[:, 0:2].astype(jnp.int32).reshape(2 * N)
    rank = route2[:, 4:6].astype(jnp.int32).reshape(2 * N)
    counts = cnt[0, N_GROUPS:N_GROUPS + N_EXPERTS].astype(jnp.int32)
    padded = (counts + R - 1) // R * R
    pad_end = jnp.cumsum(padded)
    pad_start = pad_end - padded
    dest = (pad_start[flat_e] + rank).astype(jnp.int32)
    n_blocks = (2 * N + N_EXPERTS * (R - 1) + R - 1) // R
    block_start = jnp.arange(n_blocks, dtype=jnp.int32) * R
    block_expert = jnp.minimum(jnp.sum((pad_end[None, :] <= block_start[:, None]).astype(jnp.int32), axis=1),
                               N_EXPERTS - 1).astype(jnp.int32)
    dest3 = dest.reshape(N // TOK_BLOCK, 1, 2 * TOK_BLOCK)

    xs = _moe_scatter(dest3, h2t, jnp.zeros((n_blocks * R * SUBL, LANES), F32))
    ys = _moe_experts(block_expert, xs, moe_w_gate[0], moe_w_up[0], moe_w_down[0])
    out = _moe_combine(dest3, x1.reshape(N, D), route2, mod3, ys, T // TOK_BLOCK)
    return out.reshape(B, T, D)
```
